```python
import math
import jax, jax.numpy as jnp
from jax import lax
import numpy as np

D_MODEL = 1024
BATCH = 2
SEQ = 8192
DEPTH = 2
DEC_BATCH = 1
DEC_SEQ = 16384
PAST_LEN = 128

MEM_LEN = 256
D_FF = 2816
EPS = 1e-6
N_MIXERS = 2
GLA_HEADS = 4
GLA_DK = D_MODEL // 2 // GLA_HEADS
GLA_DV = D_MODEL // GLA_HEADS
GLA_RANK = 16
GLA_TAU = 16.0
GLA_CHUNK = 64
GLA_IN = 2 * GLA_HEADS * GLA_DK + 2 * GLA_HEADS * GLA_DV + 2 * GLA_RANK
DIL_GROUPS = ((128, 1), (512, 4), (2048, 16))
N_DIL = len(DIL_GROUPS)
DIL_HEADS = 16
DIL_HD = D_MODEL // DIL_HEADS
DIL_QKV = 3 * N_DIL * DIL_HEADS * DIL_HD
NUM_BUCKETS = 32
MAX_DISTANCE = 1024
X_HEADS = 4
X_HD = D_MODEL // X_HEADS
N_A = (DEPTH + 1) // 2
N_B = DEPTH // 2
NEG = -1e30

kernel_name = 'hybrid_gla_dilated_encoder'


def rmsnorm(x, g):
    xf = x.astype(jnp.float32)
    y = xf * lax.rsqrt(jnp.mean(xf * xf, axis=-1, keepdims=True) + EPS)
    return (y * g.astype(jnp.float32)).astype(x.dtype)


def swiglu(x, w_in, w_out):
    a, b = jnp.split(x @ w_in, 2, axis=-1)
    return (jax.nn.silu(a) * b) @ w_out


def t5_bucket(rel):
    half = NUM_BUCKETS // 2
    max_exact = half // 2
    ret = (rel > 0).astype(np.int32) * half
    n = np.abs(rel)
    large = max_exact + (np.log(np.maximum(n, 1) / max_exact) / np.log(MAX_DISTANCE / max_exact) * (half - max_exact)).astype(np.int32)
    large = np.minimum(large, half - 1)
    return (ret + np.where(n < max_exact, n, large)).astype(np.int32)


def gla_scan(q, k, v, g):
    B, H, S, DK = q.shape
    DV = v.shape[-1]
    n = S // GLA_CHUNK
    causal = np.tril(np.ones((GLA_CHUNK, GLA_CHUNK), dtype=bool))

    def chunks(t):
        return jnp.moveaxis(t.reshape(B, H, n, GLA_CHUNK, t.shape[-1]), 2, 0)

    def step(state, inp):
        qc, kc, vc, gc = inp
        b = jnp.cumsum(gc, axis=2)
        inter = jnp.einsum('bhck,bhkv->bhcv', qc * jnp.exp(b), state)
        diff = b[:, :, :, None, :] - b[:, :, None, :, :]
        decay = jnp.exp(jnp.where(causal[:, :, None], diff, -jnp.inf))
        att = jnp.einsum('bhijk,bhjk->bhij', qc[:, :, :, None, :] * decay, kc)
        intra = jnp.einsum('bhij,bhjv->bhiv', att, vc)
        b_last = b[:, :, -1:, :]
        new_state = jnp.exp(b_last[:, :, 0, :])[..., None] * state + jnp.einsum('bhck,bhcv->bhkv', kc * jnp.exp(b_last - b), vc)
        return new_state, inter + intra

    state0 = jnp.zeros((B, H, DK, DV), jnp.float32)
    _, out = lax.scan(step, state0, (chunks(q), chunks(k), chunks(v), chunks(g)))
    return jnp.moveaxis(out, 0, 2).reshape(B, H, S, DV)


def gla_mixer(h, w_in, wg_f, bg_f, wg_b, bg_b, g_norm, w_out):
    B, S, _ = h.shape
    dq = GLA_HEADS * GLA_DK
    dv = GLA_HEADS * GLA_DV
    proj = h @ w_in
    q, k, v, r, zf, zb = jnp.split(proj, [dq, 2 * dq, 2 * dq + dv, 2 * dq + 2 * dv, 2 * dq + 2 * dv + GLA_RANK], axis=-1)

    def heads(t, d):
        return t.reshape(B, S, GLA_HEADS, d).transpose(0, 2, 1, 3).astype(jnp.float32)

    def log_gate(z, wg, bg):
        return heads(jax.nn.log_sigmoid((z @ wg + bg).astype(jnp.float32)) / GLA_TAU, GLA_DK)

    q = heads(q, GLA_DK) * (GLA_DK ** -0.5)
    k = heads(k, GLA_DK)
    v = heads(v, GLA_DV)
    gf = log_gate(zf, wg_f, bg_f)
    gb = log_gate(zb, wg_b, bg_b)
    flip = lambda t: jnp.flip(t, axis=2)
    o = gla_scan(q, k, v, gf) + flip(gla_scan(flip(q), flip(k), flip(v), flip(gb)))
    o = o.transpose(0, 2, 1, 3)
    o = o * lax.rsqrt(jnp.mean(o * o, axis=-1, keepdims=True) + EPS) * g_norm.reshape(GLA_HEADS, GLA_DV).astype(jnp.float32)
    o = o.reshape(B, S, dv).astype(h.dtype) * jax.nn.silu(r)
    return o @ w_out


def dilated_group(q, k, v, r, n_side, bias_cols):
    B, S, H, D = q.shape
    L = S // r
    W = n_side
    nb = -(-L // W)
    Lp = nb * W
    N = B * r

    def sub(t):
        return t.reshape(B, L, r, H, D).transpose(0, 2, 1, 3, 4).reshape(N, L, H, D)

    qs, ks, vs = sub(q), sub(k), sub(v)
    qb = jnp.pad(qs, ((0, 0), (0, Lp - L), (0, 0), (0, 0))).reshape(N, nb, W, H, D)

    def band(t):
        tp = jnp.pad(t, ((0, 0), (W, Lp - L + W), (0, 0), (0, 0))).reshape(N, nb + 2, W, H, D)
        return jnp.concatenate([tp[:, :-2], tp[:, 1:-1], tp[:, 2:]], axis=2)

    kb, vb = band(ks), band(vs)
    rel = np.arange(3 * W)[None, :] - W - np.arange(W)[:, None]
    key_pos = np.arange(nb)[:, None] * W - W + np.arange(3 * W)[None, :]
    mask = (np.abs(rel) <= W)[None] & ((key_pos >= 0) & (key_pos < L))[:, None, :]
    bias = jnp.transpose(bias_cols[t5_bucket(rel * r)], (2, 0, 1)).astype(jnp.float32)
    s = jnp.einsum('nbqhd,nbkhd->nbhqk', qb, kb).astype(jnp.float32) * (D ** -0.5) + bias
    s = jnp.where(mask[None, :, None], s, NEG)
    mx = jnp.max(s, axis=-1, keepdims=True)
    p = jnp.exp(s - mx)
    l = jnp.sum(p, axis=-1, keepdims=True)
    o = jnp.einsum('nbhqk,nbkhd->nbqhd', p, vb.astype(jnp.float32))
    o = o / jnp.moveaxis(l[..., 0], 2, 3)[..., None]
    lse = jnp.moveaxis((mx + jnp.log(l))[..., 0], 2, 3)

    def unsub(t):
        rest = t.shape[3:]
        t = t.reshape((N, Lp) + rest)[:, :L]
        return jnp.swapaxes(t.reshape((B, r, L) + rest), 1, 2).reshape((B, S) + rest)

    return unsub(o), unsub(lse)


def dilated_mixer(h, w_qkv, w_out, rel_bias):
    B, S, _ = h.shape
    qkv = (h @ w_qkv).reshape(B, S, 3, N_DIL, DIL_HEADS, DIL_HD)
    outs, lses = [], []
    for g, (window, r) in enumerate(DIL_GROUPS):
        o, lse = dilated_group(qkv[:, :, 0, g], qkv[:, :, 1, g], qkv[:, :, 2, g], r, window // (2 * r),
                               rel_bias[:, g * DIL_HEADS:(g + 1) * DIL_HEADS])
        outs.append(o)
        lses.append(lse)
    wts = jax.nn.softmax(jnp.stack(lses), axis=0)
    o = jnp.sum(wts[..., None] * jnp.stack(outs), axis=0)
    return o.reshape(B, S, D_MODEL).astype(h.dtype) @ w_out


def cross_attn(h, m, w_q, w_kv, w_o):
    B, S, _ = h.shape
    M = m.shape[1]
    q = (h @ w_q).reshape(B, S, X_HEADS, X_HD)
    kv = (m @ w_kv).reshape(B, M, 2, X_HEADS, X_HD)
    s = jnp.einsum('bqhd,bkhd->bhqk', q, kv[:, :, 0]).astype(jnp.float32) * (X_HD ** -0.5)
    p = jax.nn.softmax(s, axis=-1)
    o = jnp.einsum('bhqk,bkhd->bqhd', p, kv[:, :, 1].astype(jnp.float32))
    return o.reshape(B, S, D_MODEL).astype(h.dtype) @ w_o


def trunk(x, mem, rel_bias, norm_ffn1, ffn1_in, ffn1_out, norm_mix, gla_w_in, gla_wg_f, gla_bg_f, gla_wg_b, gla_bg_b,
          gla_norm, gla_w_out, dil_w_qkv, dil_w_out, norm_cross, norm_mem, cross_w_q, cross_w_kv, cross_w_o,
          norm_ffn2, ffn2_in, ffn2_out, norm_final):
    for i in range(DEPTH):
        x = x + 0.5 * swiglu(rmsnorm(x, norm_ffn1[i]), ffn1_in[i], ffn1_out[i])
        h = rmsnorm(x, norm_mix[i])
        j = i // N_MIXERS
        if i % N_MIXERS == 0:
            x = x + gla_mixer(h, gla_w_in[j], gla_wg_f[j], gla_bg_f[j], gla_wg_b[j], gla_bg_b[j], gla_norm[j], gla_w_out[j])
        else:
            x = x + dilated_mixer(h, dil_w_qkv[j], dil_w_out[j], rel_bias)
        x = x + cross_attn(rmsnorm(x, norm_cross[i]), rmsnorm(mem, norm_mem[i]), cross_w_q[i], cross_w_kv[i], cross_w_o[i])
        x = x + 0.5 * swiglu(rmsnorm(x, norm_ffn2[i]), ffn2_in[i], ffn2_out[i])
    return rmsnorm(x, norm_final)


def setup_inputs(seed: int = 0) -> dict:
    key = jax.random.key(seed)
    ks = jax.random.split(key, 32)
    f32 = jnp.float32

    def w(k, shape, fan_in):
        return jax.random.normal(k, shape, f32) * (fan_in ** -0.5)

    def gain(k, shape):
        return 1.0 + 0.05 * jax.random.normal(k, shape, f32)

    D = D_MODEL
    return {
        'x_prompt': jax.random.normal(ks[0], (BATCH, SEQ, D), f32),
        'x_sample': jax.random.normal(ks[1], (DEC_BATCH, DEC_SEQ, D), f32),
        'mem_prompt': jax.random.normal(ks[2], (BATCH, MEM_LEN, D), f32),
        'mem_sample': jax.random.normal(ks[3], (DEC_BATCH, MEM_LEN, D), f32),
        'rel_bias': 0.1 * jax.random.normal(ks[4], (NUM_BUCKETS, N_DIL * DIL_HEADS), f32),
        'norm_ffn1': gain(ks[5], (DEPTH, D)),
        'ffn1_in': w(ks[6], (DEPTH, D, 2 * D_FF), D),
        'ffn1_out': w(ks[7], (DEPTH, D_FF, D), D_FF),
        'norm_mix': gain(ks[8], (DEPTH, D)),
        'gla_w_in': w(ks[9], (N_A, D, GLA_IN), D),
        'gla_wg_f': w(ks[10], (N_A, GLA_RANK, GLA_HEADS * GLA_DK), GLA_RANK),
        'gla_bg_f': 0.1 * jax.random.normal(ks[11], (N_A, GLA_HEADS * GLA_DK), f32),
        'gla_wg_b': w(ks[12], (N_A, GLA_RANK, GLA_HEADS * GLA_DK), GLA_RANK),
        'gla_bg_b': 0.1 * jax.random.normal(ks[13], (N_A, GLA_HEADS * GLA_DK), f32),
        'gla_norm': gain(ks[14], (N_A, GLA_HEADS * GLA_DV)),
        'gla_w_out': w(ks[15], (N_A, GLA_HEADS * GLA_DV, D), GLA_HEADS * GLA_DV),
        'dil_w_qkv': w(ks[16], (N_B, D, DIL_QKV), D),
        'dil_w_out': w(ks[17], (N_B, DIL_HEADS * DIL_HD, D), DIL_HEADS * DIL_HD),
        'norm_cross': gain(ks[18], (DEPTH, D)),
        'norm_mem': gain(ks[19], (DEPTH, D)),
        'cross_w_q': w(ks[20], (DEPTH, D, D), D),
        'cross_w_kv': w(ks[21], (DEPTH, D, 2 * D), D),
        'cross_w_o': w(ks[22], (DEPTH, D, D), D),
        'norm_ffn2': gain(ks[23], (DEPTH, D)),
        'ffn2_in': w(ks[24], (DEPTH, D, 2 * D_FF), D),
        'ffn2_out': w(ks[25], (DEPTH, D_FF, D), D_FF),
        'norm_final': gain(ks[26], (D,)),
    }


def reference(x_prompt, x_sample, mem_prompt, mem_sample, rel_bias, norm_ffn1, ffn1_in, ffn1_out, norm_mix,
              gla_w_in, gla_wg_f, gla_bg_f, gla_wg_b, gla_bg_b, gla_norm, gla_w_out, dil_w_qkv, dil_w_out,
              norm_cross, norm_mem, cross_w_q, cross_w_kv, cross_w_o, norm_ffn2, ffn2_in, ffn2_out, norm_final):
    y_prompt = trunk(x_prompt, mem_prompt, rel_bias, norm_ffn1, ffn1_in, ffn1_out, norm_mix, gla_w_in, gla_wg_f,
                     gla_bg_f, gla_wg_b, gla_bg_b, gla_norm, gla_w_out, dil_w_qkv, dil_w_out, norm_cross, norm_mem,
                     cross_w_q, cross_w_kv, cross_w_o, norm_ffn2, ffn2_in, ffn2_out, norm_final)
    y_sample = trunk(x_sample, mem_sample, rel_bias, norm_ffn1, ffn1_in, ffn1_out, norm_mix, gla_w_in, gla_wg_f,
                     gla_bg_f, gla_wg_b, gla_bg_b, gla_norm, gla_w_out, dil_w_qkv, dil_w_out, norm_cross, norm_mem,
                     cross_w_q, cross_w_kv, cross_w_o, norm_ffn2, ffn2_in, ffn2_out, norm_final)
    return (y_prompt, y_sample)
```

```python
import functools

import numpy as np
import jax
import jax.numpy as jnp
from jax import lax
from jax.experimental import pallas as pl
from jax.experimental.pallas import tpu as pltpu

F32 = jnp.float32
BF16 = jnp.bfloat16

D_MODEL = 1024
D_FF = 2816
EPS = 1e-6
MEM_LEN = 256
GLA_HEADS = 4
GLA_DK = 128
GLA_DV = 256
GLA_RANK = 16
GLA_TAU = 16.0
DIL_GROUPS = ((128, 1), (512, 4), (2048, 16))
N_DIL = len(DIL_GROUPS)
DIL_HEADS = 16
DIL_HD = 64
DIL_W = 64
NUM_BUCKETS = 32
MAX_DISTANCE = 1024
X_HEADS = 4
X_HD = 256
NEG = -1e30

LANES = 128
VMEM_LIMIT = 56 * 1024 * 1024

TM = 256
GLA_C = 64
DIL_TQ = 256


def _params(n_axes, vmem=VMEM_LIMIT):
    return pltpu.CompilerParams(dimension_semantics=("arbitrary",) * n_axes, vmem_limit_bytes=vmem)


def _const_spec(shape):
    nd = len(shape)
    return pl.BlockSpec(shape, lambda *_: (0,) * nd, pipeline_mode=pl.Buffered(1))


def _rms(x, g):
    return x * lax.rsqrt(jnp.mean(x * x, axis=-1, keepdims=True) + EPS) * g


def _dot(a, b):
    return jnp.dot(a, b, preferred_element_type=F32)


def _dot_nt(a, b):
    return lax.dot_general(a, b, (((1,), (1,)), ((), ())), preferred_element_type=F32)


def _dot_tn(a, b):
    return lax.dot_general(a, b, (((0,), (0,)), ((), ())), preferred_element_type=F32)


def _ffn_kernel(x_ref, g_ref, win_ref, wout_ref, gfin_ref, o_ref, *, final_norm):
    x = x_ref[...]
    xn = _rms(x, g_ref[...]).astype(BF16)
    a = _dot(xn, win_ref[:, :D_FF])
    b = _dot(xn, win_ref[:, D_FF:])
    h = (a * jax.nn.sigmoid(a) * b).astype(BF16)
    y = x + 0.5 * _dot(h, wout_ref[...])
    if final_norm:
        y = _rms(y, gfin_ref[...])
    o_ref[...] = y


def _ffn(x, g, w_in, w_out, g_final, final_norm):
    t = x.shape[0]
    return pl.pallas_call(
        functools.partial(_ffn_kernel, final_norm=final_norm),
        grid=(t // TM,),
        in_specs=[
            pl.BlockSpec((TM, D_MODEL), lambda i: (i, 0)),
            _const_spec((1, D_MODEL)),
            _const_spec((D_MODEL, 2 * D_FF)),
            _const_spec((D_FF, D_MODEL)),
            _const_spec((1, D_MODEL)),
        ],
        out_specs=pl.BlockSpec((TM, D_MODEL), lambda i: (i, 0)),
        out_shape=jax.ShapeDtypeStruct((t, D_MODEL), F32),
        compiler_params=_params(1),
        name="ffn",
    )(x, g, w_in, w_out, g_final)


def _kv_kernel(m_ref, g_ref, w_ref, k_ref, v_ref):
    mn = _rms(m_ref[0], g_ref[...]).astype(BF16)
    kv = _dot(mn, w_ref[...])
    k_ref[0] = kv[:, :D_MODEL].astype(BF16)
    v_ref[0] = kv[:, D_MODEL:].astype(BF16)


def _cross_kv(mem, g, w_kv):
    nb = mem.shape[0]
    return pl.pallas_call(
        _kv_kernel,
        grid=(nb,),
        in_specs=[
            pl.BlockSpec((1, MEM_LEN, D_MODEL), lambda i: (i, 0, 0)),
            _const_spec((1, D_MODEL)),
            _const_spec((D_MODEL, 2 * D_MODEL)),
        ],
        out_specs=[pl.BlockSpec((1, MEM_LEN, D_MODEL), lambda i: (i, 0, 0))] * 2,
        out_shape=[jax.ShapeDtypeStruct((nb, MEM_LEN, D_MODEL), BF16)] * 2,
        compiler_params=_params(1),
        name="cross_kv",
    )(mem, g, w_kv)


def _cross_kernel(x_ref, g_ref, wq_ref, k_ref, v_ref, wo_ref, o_ref):
    x = x_ref[...]
    xn = _rms(x, g_ref[...]).astype(BF16)
    q = (_dot(xn, wq_ref[...]) * (X_HD ** -0.5)).astype(BF16)
    outs = []
    for h in range(X_HEADS):
        sl = slice(h * X_HD, (h + 1) * X_HD)
        s = _dot_nt(q[:, sl], k_ref[0, :, sl])
        p = jnp.exp(s - jnp.max(s, axis=-1, keepdims=True))
        l = jnp.sum(p, axis=-1, keepdims=True)
        outs.append(_dot(p.astype(BF16), v_ref[0, :, sl]) / l)
    o = jnp.concatenate(outs, axis=-1).astype(BF16)
    o_ref[...] = x + _dot(o, wo_ref[...])


def _cross(x, g, w_q, k, v, w_o, seq_bounds):
    t = x.shape[0]

    def mem_idx(i):
        tok = i * TM
        idx = 0
        for b in seq_bounds[1:-1]:
            idx = idx + (tok >= b).astype(jnp.int32)
        return (idx, 0, 0)

    return pl.pallas_call(
        _cross_kernel,
        grid=(t // TM,),
        in_specs=[
            pl.BlockSpec((TM, D_MODEL), lambda i: (i, 0)),
            _const_spec((1, D_MODEL)),
            _const_spec((D_MODEL, D_MODEL)),
            pl.BlockSpec((1, MEM_LEN, D_MODEL), mem_idx),
            pl.BlockSpec((1, MEM_LEN, D_MODEL), mem_idx),
            _const_spec((D_MODEL, D_MODEL)),
        ],
        out_specs=pl.BlockSpec((TM, D_MODEL), lambda i: (i, 0)),
        out_shape=jax.ShapeDtypeStruct((t, D_MODEL), F32),
        compiler_params=_params(1),
        name="cross_attn",
    )(x, g, w_q, k, v, w_o)


def _log_sigmoid(x):
    return jnp.minimum(x, 0.0) - jnp.log1p(jnp.exp(-jnp.abs(x)))


def _gla_proj_kernel(x_ref, g_ref, w_ref, wz_ref, wg_ref, bg_ref,
                     q_ref, k_ref, v_ref, gate_ref, gf_ref, gb_ref):
    dq = GLA_HEADS * GLA_DK
    dv = GLA_HEADS * GLA_DV
    xn = _rms(x_ref[...], g_ref[...]).astype(BF16)
    q_ref[...] = _dot(xn, w_ref[:, :dq]) * (GLA_DK ** -0.5)
    k_ref[...] = _dot(xn, w_ref[:, dq:2 * dq])
    v_ref[...] = _dot(xn, w_ref[:, 2 * dq:2 * dq + dv]).astype(BF16)
    r = _dot(xn, w_ref[:, 2 * dq + dv:])
    gate_ref[...] = r * jax.nn.sigmoid(r)
    z = _dot(xn, wz_ref[...]).astype(BF16)
    pre = _dot(z, wg_ref[...]) + bg_ref[...]
    lg = _log_sigmoid(pre) * (1.0 / GLA_TAU)
    gf_ref[...] = lg[:, :dq]
    gb_ref[...] = lg[:, dq:]


def _gla_proj(x, g, w_main, w_z, w_g, b_g):
    t = x.shape[0]
    dq = GLA_HEADS * GLA_DK
    dv = GLA_HEADS * GLA_DV
    row = lambda n: pl.BlockSpec((TM, n), lambda i: (i, 0))
    return pl.pallas_call(
        _gla_proj_kernel,
        grid=(t // TM,),
        in_specs=[
            row(D_MODEL),
            _const_spec((1, D_MODEL)),
            _const_spec((D_MODEL, 2 * dq + 2 * dv)),
            _const_spec((D_MODEL, 2 * GLA_RANK)),
            _const_spec((2 * GLA_RANK, 2 * dq)),
            _const_spec((1, 2 * dq)),
        ],
        out_specs=[row(dq), row(dq), row(dv), row(dv), row(dq), row(dq)],
        out_shape=[
            jax.ShapeDtypeStruct((t, dq), F32),
            jax.ShapeDtypeStruct((t, dq), F32),
            jax.ShapeDtypeStruct((t, dv), BF16),
            jax.ShapeDtypeStruct((t, dv), F32),
            jax.ShapeDtypeStruct((t, dq), F32),
            jax.ShapeDtypeStruct((t, dq), F32),
        ],
        compiler_params=_params(1),
        name="gla_proj",
    )(x, g, w_main, w_z, w_g, b_g)


def _gla_tables(c):
    n_lev = c.bit_length() - 1
    idx = np.arange(c)
    mats = [(idx[None, :] <= idx[:, None]), (idx[None, :] > idx[:, None])]
    qside = []
    for l in range(n_lev):
        m = c >> (l + 1)
        blk = idx // m
        odd = (blk % 2 == 1)
        start = blk * m
        end = (blk + 1) * m - 1
        t = idx[None, :]
        mq = (t >= start[:, None]) & (t <= idx[:, None])
        mk = (t > idx[:, None]) & (t <= end[:, None])
        mats.append(np.where(odd[:, None], mq, mk))
        qside.append(odd)
    m_f = np.concatenate(mats, axis=0).astype(np.float32)
    level = np.full((c, c), -2, np.int32)
    level[idx, idx] = -1
    for i in range(c):
        for j in range(i):
            level[i, j] = n_lev - (i ^ j).bit_length()
    qs_f = np.stack(qside).astype(np.float32)
    m_b = m_f.reshape(2 + n_lev, c, c)[:, ::-1, ::-1].reshape(-1, c)
    level_b = level[::-1, ::-1]
    qs_b = qs_f[:, ::-1]
    m_all = np.stack([m_f, m_b])
    lvl_all = np.stack([level, level_b])
    qs_all = np.broadcast_to(np.stack([qs_f, qs_b])[..., None], (2, n_lev, c, LANES))
    return (jnp.asarray(m_all, BF16), jnp.asarray(lvl_all, jnp.int32), jnp.asarray(qs_all, F32))


def _gla_scan_kernel(m_ref, lvl_ref, qs_ref,
                     qf_ref, kf_ref, vf_ref, gf_ref,
                     qb_ref, kb_ref, vb_ref, gb_ref,
                     of_ref, ob_ref, state_ref, *, n_chunks, resets_f, resets_b):
    c = GLA_C
    n_lev = c.bit_length() - 1
    step = pl.program_id(0)

    def is_any(vals):
        cond = step == vals[0]
        for v in vals[1:]:
            cond = jnp.logical_or(cond, step == v)
        return cond

    @pl.when(is_any(resets_f))
    def _():
        state_ref[0] = jnp.zeros(state_ref.shape[1:], F32)

    @pl.when(is_any(resets_b))
    def _():
        state_ref[1] = jnp.zeros(state_ref.shape[1:], F32)

    dirs = ((0, qf_ref, kf_ref, vf_ref, gf_ref, of_ref, c - 1),
            (1, qb_ref, kb_ref, vb_ref, gb_ref, ob_ref, 0))
    for d, q_ref, k_ref, v_ref, g_ref, o_ref, last_row in dirs:
        mm = m_ref[d]
        lvl = lvl_ref[d]
        for h in range(GLA_HEADS):
            ks = slice(h * GLA_DK, (h + 1) * GLA_DK)
            vs = slice(h * GLA_DV, (h + 1) * GLA_DV)
            q = q_ref[:, ks]
            k = k_ref[:, ks]
            v = v_ref[:, vs]
            g = g_ref[:, ks]
            g1 = g.astype(BF16)
            r1 = g - g1.astype(F32)
            g2 = r1.astype(BF16)
            g3 = (r1 - g2.astype(F32)).astype(BF16)
            e = jnp.exp(_dot(mm, g1) + _dot(mm, g2) + _dot(mm, g3))
            eb = e[0:c]
            ebl = e[c:2 * c]
            a = jnp.where(lvl == -1, jnp.sum(q * k, axis=-1, keepdims=True), 0.0)
            for l in range(n_lev):
                w = (jnp.where(qs_ref[d, l] > 0.5, q, k) * e[(2 + l) * c:(3 + l) * c]).astype(BF16)
                a = jnp.where(lvl == l, _dot_nt(w, w), a)
            st = state_ref[d, h]
            o = _dot_nt((q * eb).astype(BF16), st.astype(BF16)) + _dot(a.astype(BF16), v)
            o_ref[:, vs] = o
            upd = _dot_tn(v, (k * ebl).astype(BF16))
            state_ref[d, h] = st * eb[last_row:last_row + 1, :] + upd


def _gla_scan(q, k, v, gf, gb, seq_bounds):
    t = q.shape[0]
    c = GLA_C
    n_chunks = t // c
    dq = GLA_HEADS * GLA_DK
    dv = GLA_HEADS * GLA_DV
    m_all, lvl_all, qs_all = _gla_tables(c)
    resets_f = tuple(b // c for b in seq_bounds[:-1])
    resets_b = tuple(n_chunks - b // c for b in seq_bounds[1:])
    fwd = lambda n: pl.BlockSpec((c, n), lambda i: (i, 0))
    bwd = lambda n: pl.BlockSpec((c, n), lambda i: (n_chunks - 1 - i, 0))
    return pl.pallas_call(
        functools.partial(_gla_scan_kernel, n_chunks=n_chunks, resets_f=resets_f, resets_b=resets_b),
        grid=(n_chunks,),
        in_specs=[
            _const_spec(m_all.shape), _const_spec(lvl_all.shape), _const_spec(qs_all.shape),
            fwd(dq), fwd(dq), fwd(dv), fwd(dq),
            bwd(dq), bwd(dq), bwd(dv), bwd(dq),
        ],
        out_specs=[fwd(dv), bwd(dv)],
        out_shape=[jax.ShapeDtypeStruct((t, dv), F32)] * 2,
        scratch_shapes=[pltpu.VMEM((2, GLA_HEADS, GLA_DV, GLA_DK), F32)],
        compiler_params=_params(1),
        name="gla_scan",
    )(m_all, lvl_all, qs_all, q, k, v, gf, q, k, v, gb)


def _gla_out_kernel(x_ref, of_ref, ob_ref, gate_ref, gn_ref, w_ref, o_ref):
    o = of_ref[...] + ob_ref[...]
    outs = []
    for h in range(GLA_HEADS):
        sl = slice(h * GLA_DV, (h + 1) * GLA_DV)
        outs.append(_rms(o[:, sl], gn_ref[:, sl]))
    y = (jnp.concatenate(outs, axis=-1) * gate_ref[...]).astype(BF16)
    o_ref[...] = x_ref[...] + _dot(y, w_ref[...])


def _gla_out(x, o_f, o_b, gate, g_norm, w_out):
    t = x.shape[0]
    row = pl.BlockSpec((TM, D_MODEL), lambda i: (i, 0))
    return pl.pallas_call(
        _gla_out_kernel,
        grid=(t // TM,),
        in_specs=[row, row, row, row, _const_spec((1, D_MODEL)), _const_spec((D_MODEL, D_MODEL))],
        out_specs=row,
        out_shape=jax.ShapeDtypeStruct((t, D_MODEL), F32),
        compiler_params=_params(1),
        name="gla_out",
    )(x, o_f, o_b, gate, g_norm, w_out)


def _dil_proj_kernel(x_ref, g_ref, w_ref, o_ref):
    xn = _rms(x_ref[...], g_ref[...]).astype(BF16)
    n_q = N_DIL * D_MODEL
    for j in range(3 * N_DIL):
        sl = slice(j * D_MODEL, (j + 1) * D_MODEL)
        y = _dot(xn, w_ref[:, sl])
        if j * D_MODEL < n_q:
            y = y * (DIL_HD ** -0.5)
        o_ref[:, sl] = y.astype(BF16)


def _dil_proj(x, g, w_qkv):
    t = x.shape[0]
    n = 3 * N_DIL * D_MODEL
    return pl.pallas_call(
        _dil_proj_kernel,
        grid=(t // TM,),
        in_specs=[pl.BlockSpec((TM, D_MODEL), lambda i: (i, 0)), _const_spec((1, D_MODEL)),
                  _const_spec((D_MODEL, n))],
        out_specs=pl.BlockSpec((TM, n), lambda i: (i, 0)),
        out_shape=jax.ShapeDtypeStruct((t, n), BF16),
        compiler_params=_params(1),
        name="dil_proj",
    )(x, g, w_qkv)


def _t5_bucket(rel):
    half = NUM_BUCKETS // 2
    max_exact = half // 2
    ret = (rel > 0).astype(np.int32) * half
    n = np.abs(rel)
    large = max_exact + (np.log(np.maximum(n, 1) / max_exact) / np.log(MAX_DISTANCE / max_exact)
                         * (half - max_exact)).astype(np.int32)
    large = np.minimum(large, half - 1)
    return (ret + np.where(n < max_exact, n, large)).astype(np.int32)


def _dil_bias_table(rel_bias_cols, r):
    tq, w = DIL_TQ, DIL_W
    rel = np.arange(tq + 2 * w)[None, :] - w - np.arange(tq)[:, None]
    band = np.abs(rel) <= w
    bucket = _t5_bucket(np.clip(rel, -w, w) * r)
    bias = jnp.transpose(rel_bias_cols[bucket], (2, 0, 1)).astype(F32)
    return jnp.where(band[None], bias, NEG)


def _dil_attn_kernel(bias_ref, q_ref, kp_ref, km_ref, kn_ref, vp_ref, vm_ref, vn_ref,
                     o_ref, lse_ref, kx_ref, vx_ref, *, sub_bounds):
    tq, w = DIL_TQ, DIL_W
    tk = tq + 2 * w
    i = pl.program_id(1)
    kx_ref[0:w] = kp_ref[...]
    kx_ref[w:w + tq] = km_ref[...]
    kx_ref[w + tq:] = kn_ref[...]
    vx_ref[0:w] = vp_ref[...]
    vx_ref[w:w + tq] = vm_ref[...]
    vx_ref[w + tq:] = vn_ref[...]
    start = i * tq
    lo = jnp.int32(0)
    hi = jnp.int32(sub_bounds[1])
    for b0, b1 in zip(sub_bounds[1:-1], sub_bounds[2:]):
        inside = start >= b0
        lo = jnp.where(inside, b0, lo)
        hi = jnp.where(inside, b1, hi)
    kpos = start - w + lax.broadcasted_iota(jnp.int32, (1, tk), 1)
    edge = jnp.where((kpos >= lo) & (kpos < hi), 0.0, NEG)

    lane = lax.broadcasted_iota(jnp.int32, (tq, LANES), 1)
    low_half = lane < DIL_HD
    lse_acc = jnp.zeros((tq, LANES), F32)
    for p in range(DIL_HEADS // 2):
        sl = slice(p * LANES, (p + 1) * LANES)
        q2 = q_ref[:, sl]
        k2 = kx_ref[:, sl]
        v2 = vx_ref[:, sl]
        zero = jnp.zeros_like(q2)
        halves = []
        for half in range(2):
            h = 2 * p + half
            qh = jnp.where(low_half, q2, zero) if half == 0 else jnp.where(low_half, zero, q2)
            s = _dot_nt(qh, k2) + bias_ref[h] + edge
            mx = jnp.max(s, axis=-1, keepdims=True)
            pr = jnp.exp(s - mx)
            l = jnp.sum(pr, axis=-1, keepdims=True)
            halves.append(_dot(pr.astype(BF16), v2) / l)
            lse_acc = jnp.where(lane == h, mx + jnp.log(l), lse_acc)
        o_ref[:, sl] = jnp.where(low_half, halves[0], halves[1])
    lse_ref[...] = lse_acc


def _dil_attn(qkv, bias, g, r, seq_bounds):
    t = qkv.shape[0]
    tq, w = DIL_TQ, DIL_W
    rows = t // r
    ncol = 3 * N_DIL
    qkv_v = qkv.reshape(rows, r * ncol * D_MODEL)
    n_blk = rows // tq
    n_w = rows // w
    per = tq // w
    sub_bounds = tuple(b // r for b in seq_bounds)
    assert all(b % tq == 0 for b in sub_bounds)

    def main(col):
        return pl.BlockSpec((tq, D_MODEL), lambda rho, i: (i, rho * ncol + col))

    def prev(col):
        return pl.BlockSpec((w, D_MODEL), lambda rho, i: (jnp.maximum(i * per - 1, 0), rho * ncol + col))

    def nxt(col):
        return pl.BlockSpec((w, D_MODEL), lambda rho, i: (jnp.minimum((i + 1) * per, n_w - 1), rho * ncol + col))

    kc, vc = N_DIL + g, 2 * N_DIL + g
    o, lse = pl.pallas_call(
        functools.partial(_dil_attn_kernel, sub_bounds=sub_bounds),
        grid=(r, n_blk),
        in_specs=[_const_spec(bias.shape), main(g), prev(kc), main(kc), nxt(kc), prev(vc), main(vc), nxt(vc)],
        out_specs=[pl.BlockSpec((tq, D_MODEL), lambda rho, i: (i, rho)),
                   pl.BlockSpec((tq, LANES), lambda rho, i: (i, rho))],
        out_shape=[jax.ShapeDtypeStruct((rows, r * D_MODEL), F32),
                   jax.ShapeDtypeStruct((rows, r * LANES), F32)],
        scratch_shapes=[pltpu.VMEM((tq + 2 * w, D_MODEL), BF16)] * 2,
        compiler_params=_params(2),
        name=f"dil_attn_r{r}",
    )(bias, qkv_v, qkv_v, qkv_v, qkv_v, qkv_v, qkv_v, qkv_v)
    return o.reshape(t, D_MODEL), lse.reshape(t, LANES)


def _dil_out_kernel(x_ref, o0_ref, o1_ref, o2_ref, l0_ref, l1_ref, l2_ref, ex_ref, w_ref, out_ref):
    o_refs = (o0_ref, o1_ref, o2_ref)
    lses = [l0_ref[...], l1_ref[...], l2_ref[...]]
    mx = jnp.maximum(jnp.maximum(lses[0], lses[1]), lses[2])
    es = [jnp.exp(l - mx) for l in lses]
    inv = 1.0 / (es[0] + es[1] + es[2])
    ex = ex_ref[...]
    acc = None
    for e, o_ref in zip(es, o_refs):
        wt = e * inv
        hi = wt.astype(BF16)
        lo = (wt - hi.astype(F32)).astype(BF16)
        term = (_dot(hi, ex) + _dot(lo, ex)) * o_ref[...]
        acc = term if acc is None else acc + term
    out_ref[...] = x_ref[...] + _dot(acc.astype(BF16), w_ref[...])


def _dil_out(x, outs, lses, w_out):
    t = x.shape[0]
    expand = np.zeros((LANES, D_MODEL), np.float32)
    for h in range(DIL_HEADS):
        expand[h, h * DIL_HD:(h + 1) * DIL_HD] = 1.0
    row = pl.BlockSpec((TM, D_MODEL), lambda i: (i, 0))
    lrow = pl.BlockSpec((TM, LANES), lambda i: (i, 0))
    return pl.pallas_call(
        _dil_out_kernel,
        grid=(t // TM,),
        in_specs=[row, row, row, row, lrow, lrow, lrow,
                  _const_spec((LANES, D_MODEL)), _const_spec((D_MODEL, D_MODEL))],
        out_specs=row,
        out_shape=jax.ShapeDtypeStruct((t, D_MODEL), F32),
        compiler_params=_params(1),
        name="dil_out",
    )(x, *outs, *lses, jnp.asarray(expand, BF16), w_out)


def kernel(x_prompt, x_sample, mem_prompt, mem_sample, rel_bias, norm_ffn1, ffn1_in, ffn1_out, norm_mix,
           gla_w_in, gla_wg_f, gla_bg_f, gla_wg_b, gla_bg_b, gla_norm, gla_w_out, dil_w_qkv, dil_w_out,
           norm_cross, norm_mem, cross_w_q, cross_w_kv, cross_w_o, norm_ffn2, ffn2_in, ffn2_out, norm_final):
    bp, sp, _ = x_prompt.shape
    bs, ss, _ = x_sample.shape
    depth = norm_ffn1.shape[0]
    x = jnp.concatenate([x_prompt.reshape(bp * sp, D_MODEL), x_sample.reshape(bs * ss, D_MODEL)], axis=0)
    mem = jnp.concatenate([mem_prompt, mem_sample], axis=0)
    seq_bounds = tuple(i * sp for i in range(bp)) + tuple(bp * sp + i * ss for i in range(bs + 1))
    row = lambda v: v.reshape(1, -1)
    bf = lambda v: v.astype(BF16)
    dq = GLA_HEADS * GLA_DK
    dv = GLA_HEADS * GLA_DV

    for i in range(depth):
        x = _ffn(x, row(norm_ffn1[i]), bf(ffn1_in[i]), bf(ffn1_out[i]), row(norm_final), False)
        j = i // 2
        if i % 2 == 0:
            w_in = gla_w_in[j]
            zero = jnp.zeros((GLA_RANK, dq), F32)
            w_g = jnp.concatenate([jnp.concatenate([gla_wg_f[j], zero], axis=1),
                                   jnp.concatenate([zero, gla_wg_b[j]], axis=1)], axis=0)
            b_g = jnp.concatenate([gla_bg_f[j], gla_bg_b[j]]).reshape(1, 2 * dq)
            q, k, v, gate, gf, gb = _gla_proj(x, row(norm_mix[i]), bf(w_in[:, :2 * dq + 2 * dv]),
                                              bf(w_in[:, 2 * dq + 2 * dv:]), bf(w_g), b_g)
            o_f, o_b = _gla_scan(q, k, v, gf, gb, seq_bounds)
            x = _gla_out(x, o_f, o_b, gate, row(gla_norm[j]), bf(gla_w_out[j]))
        else:
            qkv = _dil_proj(x, row(norm_mix[i]), bf(dil_w_qkv[j]))
            outs, lses = [], []
            for g, (_, r) in enumerate(DIL_GROUPS):
                bias = _dil_bias_table(rel_bias[:, g * DIL_HEADS:(g + 1) * DIL_HEADS], r)
                o, lse = _dil_attn(qkv, bias, g, r, seq_bounds)
                outs.append(o)
                lses.append(lse)
            x = _dil_out(x, outs, lses, bf(dil_w_out[j]))
        k_mem, v_mem = _cross_kv(mem, row(norm_mem[i]), bf(cross_w_kv[i]))
        x = _cross(x, row(norm_cross[i]), bf(cross_w_q[i]), k_mem, v_mem, bf(cross_w_o[i]), seq_bounds)
        x = _ffn(x, row(norm_ffn2[i]), bf(ffn2_in[i]), bf(ffn2_out[i]), row(norm_final), i == depth - 1)

    y_prompt = x[:bp * sp].reshape(bp, sp, D_MODEL)
    y_sample = x[bp * sp:].reshape(bs, ss, D_MODEL)
    return (y_prompt, y_sample)
```

```python
import functools

import numpy as np
import jax
import jax.numpy as jnp
from jax import lax
from jax.experimental import pallas as pl
from jax.experimental.pallas import tpu as pltpu

F32 = jnp.float32
BF16 = jnp.bfloat16

D_MODEL = 1024
D_FF = 2816
EPS = 1e-6
MEM_LEN = 256
GLA_HEADS = 4
GLA_DK = 128
GLA_DV = 256
GLA_RANK = 16
GLA_TAU = 16.0
DIL_GROUPS = ((128, 1), (512, 4), (2048, 16))
N_DIL = len(DIL_GROUPS)
DIL_HEADS = 16
DIL_HD = 64
DIL_W = 64
NUM_BUCKETS = 32
MAX_DISTANCE = 1024
X_HEADS = 4
X_HD = 256
NEG = -1e30

LANES = 128
VMEM_LIMIT = 56 * 1024 * 1024

TM = 256
GLA_C = 64
DIL_TQ = 256
DIL_QB = 128
DIL_HG = 4


def _params(n_axes, vmem=VMEM_LIMIT):
    return pltpu.CompilerParams(dimension_semantics=("arbitrary",) * n_axes, vmem_limit_bytes=vmem)


def _const_spec(shape):
    nd = len(shape)
    return pl.BlockSpec(shape, lambda *_: (0,) * nd, pipeline_mode=pl.Buffered(1))


def _rms(x, g):
    return x * lax.rsqrt(jnp.mean(x * x, axis=-1, keepdims=True) + EPS) * g


def _dot(a, b):
    return jnp.dot(a, b, preferred_element_type=F32)


def _dot_nt(a, b):
    return lax.dot_general(a, b, (((1,), (1,)), ((), ())), preferred_element_type=F32)


def _dot_tn(a, b):
    return lax.dot_general(a, b, (((0,), (0,)), ((), ())), preferred_element_type=F32)


def _ffn_kernel(x_ref, g_ref, win_ref, wout_ref, gfin_ref, o_ref, *, final_norm):
    x = x_ref[...]
    xn = _rms(x, g_ref[...]).astype(BF16)
    a = _dot(xn, win_ref[:, :D_FF])
    b = _dot(xn, win_ref[:, D_FF:])
    h = (a * jax.nn.sigmoid(a) * b).astype(BF16)
    y = x + 0.5 * _dot(h, wout_ref[...])
    if final_norm:
        y = _rms(y, gfin_ref[...])
    o_ref[...] = y


def _ffn(x, g, w_in, w_out, g_final, final_norm):
    t = x.shape[0]
    return pl.pallas_call(
        functools.partial(_ffn_kernel, final_norm=final_norm),
        grid=(t // TM,),
        in_specs=[
            pl.BlockSpec((TM, D_MODEL), lambda i: (i, 0)),
            _const_spec((1, D_MODEL)),
            _const_spec((D_MODEL, 2 * D_FF)),
            _const_spec((D_FF, D_MODEL)),
            _const_spec((1, D_MODEL)),
        ],
        out_specs=pl.BlockSpec((TM, D_MODEL), lambda i: (i, 0)),
        out_shape=jax.ShapeDtypeStruct((t, D_MODEL), F32),
        compiler_params=_params(1),
        name="ffn",
    )(x, g, w_in, w_out, g_final)


def _kv_kernel(m_ref, g_ref, w_ref, k_ref, v_ref):
    mn = _rms(m_ref[0], g_ref[...]).astype(BF16)
    kv = _dot(mn, w_ref[...])
    k_ref[0] = kv[:, :D_MODEL].astype(BF16)
    v_ref[0] = kv[:, D_MODEL:].astype(BF16)


def _cross_kv(mem, g, w_kv):
    nb = mem.shape[0]
    return pl.pallas_call(
        _kv_kernel,
        grid=(nb,),
        in_specs=[
            pl.BlockSpec((1, MEM_LEN, D_MODEL), lambda i: (i, 0, 0)),
            _const_spec((1, D_MODEL)),
            _const_spec((D_MODEL, 2 * D_MODEL)),
        ],
        out_specs=[pl.BlockSpec((1, MEM_LEN, D_MODEL), lambda i: (i, 0, 0))] * 2,
        out_shape=[jax.ShapeDtypeStruct((nb, MEM_LEN, D_MODEL), BF16)] * 2,
        compiler_params=_params(1),
        name="cross_kv",
    )(mem, g, w_kv)


def _cross_kernel(x_ref, g_ref, wq_ref, k_ref, v_ref, wo_ref, o_ref):
    x = x_ref[...]
    xn = _rms(x, g_ref[...]).astype(BF16)
    q = (_dot(xn, wq_ref[...]) * (X_HD ** -0.5)).astype(BF16)
    outs = []
    for h in range(X_HEADS):
        sl = slice(h * X_HD, (h + 1) * X_HD)
        s = _dot_nt(q[:, sl], k_ref[0, :, sl])
        p = jnp.exp(s - jnp.max(s, axis=-1, keepdims=True))
        l = jnp.sum(p, axis=-1, keepdims=True)
        outs.append(_dot(p.astype(BF16), v_ref[0, :, sl]) / l)
    o = jnp.concatenate(outs, axis=-1).astype(BF16)
    o_ref[...] = x + _dot(o, wo_ref[...])


def _cross(x, g, w_q, k, v, w_o, seq_bounds):
    t = x.shape[0]

    def mem_idx(i):
        tok = i * TM
        idx = 0
        for b in seq_bounds[1:-1]:
            idx = idx + (tok >= b).astype(jnp.int32)
        return (idx, 0, 0)

    return pl.pallas_call(
        _cross_kernel,
        grid=(t // TM,),
        in_specs=[
            pl.BlockSpec((TM, D_MODEL), lambda i: (i, 0)),
            _const_spec((1, D_MODEL)),
            _const_spec((D_MODEL, D_MODEL)),
            pl.BlockSpec((1, MEM_LEN, D_MODEL), mem_idx),
            pl.BlockSpec((1, MEM_LEN, D_MODEL), mem_idx),
            _const_spec((D_MODEL, D_MODEL)),
        ],
        out_specs=pl.BlockSpec((TM, D_MODEL), lambda i: (i, 0)),
        out_shape=jax.ShapeDtypeStruct((t, D_MODEL), F32),
        compiler_params=_params(1),
        name="cross_attn",
    )(x, g, w_q, k, v, w_o)


def _log_sigmoid(x):
    return jnp.minimum(x, 0.0) - jnp.log1p(jnp.exp(-jnp.abs(x)))


def _gla_proj_kernel(x_ref, g_ref, w_ref, wz_ref, wg_ref, bg_ref,
                     q_ref, k_ref, v_ref, gate_ref, gf_ref, gb_ref):
    dq = GLA_HEADS * GLA_DK
    dv = GLA_HEADS * GLA_DV
    xn = _rms(x_ref[...], g_ref[...]).astype(BF16)
    q_ref[...] = _dot(xn, w_ref[:, :dq]) * (GLA_DK ** -0.5)
    k_ref[...] = _dot(xn, w_ref[:, dq:2 * dq])
    v_ref[...] = _dot(xn, w_ref[:, 2 * dq:2 * dq + dv]).astype(BF16)
    r = _dot(xn, w_ref[:, 2 * dq + dv:])
    gate_ref[...] = r * jax.nn.sigmoid(r)
    z = _dot(xn, wz_ref[...]).astype(BF16)
    pre = _dot(z, wg_ref[...]) + bg_ref[...]
    lg = _log_sigmoid(pre) * (1.0 / GLA_TAU)
    gf_ref[...] = lg[:, :dq]
    gb_ref[...] = lg[:, dq:]


def _gla_proj(x, g, w_main, w_z, w_g, b_g):
    t = x.shape[0]
    dq = GLA_HEADS * GLA_DK
    dv = GLA_HEADS * GLA_DV
    row = lambda n: pl.BlockSpec((TM, n), lambda i: (i, 0))
    return pl.pallas_call(
        _gla_proj_kernel,
        grid=(t // TM,),
        in_specs=[
            row(D_MODEL),
            _const_spec((1, D_MODEL)),
            _const_spec((D_MODEL, 2 * dq + 2 * dv)),
            _const_spec((D_MODEL, 2 * GLA_RANK)),
            _const_spec((2 * GLA_RANK, 2 * dq)),
            _const_spec((1, 2 * dq)),
        ],
        out_specs=[row(dq), row(dq), row(dv), row(dv), row(dq), row(dq)],
        out_shape=[
            jax.ShapeDtypeStruct((t, dq), F32),
            jax.ShapeDtypeStruct((t, dq), F32),
            jax.ShapeDtypeStruct((t, dv), BF16),
            jax.ShapeDtypeStruct((t, dv), F32),
            jax.ShapeDtypeStruct((t, dq), F32),
            jax.ShapeDtypeStruct((t, dq), F32),
        ],
        compiler_params=_params(1),
        name="gla_proj",
    )(x, g, w_main, w_z, w_g, b_g)


def _gla_tables(c):
    n_lev = c.bit_length() - 1
    idx = np.arange(c)
    mats = [(idx[None, :] <= idx[:, None]), (idx[None, :] > idx[:, None])]
    qside = []
    for l in range(n_lev):
        m = c >> (l + 1)
        blk = idx // m
        odd = (blk % 2 == 1)
        start = blk * m
        end = (blk + 1) * m - 1
        t = idx[None, :]
        mq = (t >= start[:, None]) & (t <= idx[:, None])
        mk = (t > idx[:, None]) & (t <= end[:, None])
        mats.append(np.where(odd[:, None], mq, mk))
        qside.append(odd)
    m_f = np.concatenate(mats, axis=0).astype(np.float32)
    level = np.full((c, c), -2, np.int32)
    level[idx, idx] = -1
    for i in range(c):
        for j in range(i):
            level[i, j] = n_lev - (i ^ j).bit_length()
    qs_f = np.stack(qside).astype(np.float32)
    m_b = m_f.reshape(2 + n_lev, c, c)[:, ::-1, ::-1].reshape(-1, c)
    level_b = level[::-1, ::-1]
    qs_b = qs_f[:, ::-1]
    m_all = np.stack([m_f, m_b])
    lvl_all = np.stack([level, level_b])
    qs_all = np.broadcast_to(np.stack([qs_f, qs_b])[..., None], (2, n_lev, c, LANES))
    return (jnp.asarray(m_all, BF16), jnp.asarray(lvl_all, jnp.int32), jnp.asarray(qs_all, F32))


def _gla_scan_kernel(m_ref, lvl_ref, qs_ref,
                     qf_ref, kf_ref, vf_ref, gf_ref,
                     qb_ref, kb_ref, vb_ref, gb_ref,
                     of_ref, ob_ref, state_ref, *, n_chunks, resets_f, resets_b):
    c = GLA_C
    n_lev = c.bit_length() - 1
    step = pl.program_id(0)

    def is_any(vals):
        cond = step == vals[0]
        for v in vals[1:]:
            cond = jnp.logical_or(cond, step == v)
        return cond

    @pl.when(is_any(resets_f))
    def _():
        state_ref[0] = jnp.zeros(state_ref.shape[1:], F32)

    @pl.when(is_any(resets_b))
    def _():
        state_ref[1] = jnp.zeros(state_ref.shape[1:], F32)

    dirs = ((0, qf_ref, kf_ref, vf_ref, gf_ref, of_ref, c - 1),
            (1, qb_ref, kb_ref, vb_ref, gb_ref, ob_ref, 0))
    for d, q_ref, k_ref, v_ref, g_ref, o_ref, last_row in dirs:
        mm = m_ref[d]
        lvl = lvl_ref[d]
        for h in range(GLA_HEADS):
            ks = slice(h * GLA_DK, (h + 1) * GLA_DK)
            vs = slice(h * GLA_DV, (h + 1) * GLA_DV)
            q = q_ref[:, ks]
            k = k_ref[:, ks]
            v = v_ref[:, vs]
            g = g_ref[:, ks]
            g1 = g.astype(BF16)
            r1 = g - g1.astype(F32)
            g2 = r1.astype(BF16)
            g3 = (r1 - g2.astype(F32)).astype(BF16)
            e = jnp.exp(_dot(mm, g1) + _dot(mm, g2) + _dot(mm, g3))
            eb = e[0:c]
            ebl = e[c:2 * c]
            a = jnp.where(lvl == -1, jnp.sum(q * k, axis=-1, keepdims=True), 0.0)
            for l in range(n_lev):
                w = (jnp.where(qs_ref[d, l] > 0.5, q, k) * e[(2 + l) * c:(3 + l) * c]).astype(BF16)
                a = jnp.where(lvl == l, _dot_nt(w, w), a)
            st = state_ref[d, h]
            o = _dot_nt((q * eb).astype(BF16), st.astype(BF16)) + _dot(a.astype(BF16), v)
            o_ref[:, vs] = o
            upd = _dot_tn(v, (k * ebl).astype(BF16))
            state_ref[d, h] = st * eb[last_row:last_row + 1, :] + upd


def _gla_scan(q, k, v, gf, gb, seq_bounds):
    t = q.shape[0]
    c = GLA_C
    n_chunks = t // c
    dq = GLA_HEADS * GLA_DK
    dv = GLA_HEADS * GLA_DV
    m_all, lvl_all, qs_all = _gla_tables(c)
    resets_f = tuple(b // c for b in seq_bounds[:-1])
    resets_b = tuple(n_chunks - b // c for b in seq_bounds[1:])
    fwd = lambda n: pl.BlockSpec((c, n), lambda i: (i, 0))
    bwd = lambda n: pl.BlockSpec((c, n), lambda i: (n_chunks - 1 - i, 0))
    return pl.pallas_call(
        functools.partial(_gla_scan_kernel, n_chunks=n_chunks, resets_f=resets_f, resets_b=resets_b),
        grid=(n_chunks,),
        in_specs=[
            _const_spec(m_all.shape), _const_spec(lvl_all.shape), _const_spec(qs_all.shape),
            fwd(dq), fwd(dq), fwd(dv), fwd(dq),
            bwd(dq), bwd(dq), bwd(dv), bwd(dq),
        ],
        out_specs=[fwd(dv), bwd(dv)],
        out_shape=[jax.ShapeDtypeStruct((t, dv), F32)] * 2,
        scratch_shapes=[pltpu.VMEM((2, GLA_HEADS, GLA_DV, GLA_DK), F32)],
        compiler_params=_params(1),
        name="gla_scan",
    )(m_all, lvl_all, qs_all, q, k, v, gf, q, k, v, gb)


def _gla_out_kernel(x_ref, of_ref, ob_ref, gate_ref, gn_ref, w_ref, o_ref):
    o = of_ref[...] + ob_ref[...]
    outs = []
    for h in range(GLA_HEADS):
        sl = slice(h * GLA_DV, (h + 1) * GLA_DV)
        outs.append(_rms(o[:, sl], gn_ref[:, sl]))
    y = (jnp.concatenate(outs, axis=-1) * gate_ref[...]).astype(BF16)
    o_ref[...] = x_ref[...] + _dot(y, w_ref[...])


def _gla_out(x, o_f, o_b, gate, g_norm, w_out):
    t = x.shape[0]
    row = pl.BlockSpec((TM, D_MODEL), lambda i: (i, 0))
    return pl.pallas_call(
        _gla_out_kernel,
        grid=(t // TM,),
        in_specs=[row, row, row, row, _const_spec((1, D_MODEL)), _const_spec((D_MODEL, D_MODEL))],
        out_specs=row,
        out_shape=jax.ShapeDtypeStruct((t, D_MODEL), F32),
        compiler_params=_params(1),
        name="gla_out",
    )(x, o_f, o_b, gate, g_norm, w_out)


def _dil_proj_kernel(x_ref, g_ref, w_ref, o0_ref, o1_ref, o2_ref, xs_ref):
    o_refs = (o0_ref, o1_ref, o2_ref)
    n_chunk = D_MODEL // LANES
    for j in range(n_chunk):
        xs_ref[j] = x_ref[:, j * LANES:(j + 1) * LANES]
    for g, (_, r) in enumerate(DIL_GROUPS):
        n = TM // r
        if r == 1:
            xr = x_ref[...]
        else:
            xr = jnp.concatenate(
                [jnp.concatenate([xs_ref[j, pl.ds(rho, n, stride=r), :] for j in range(n_chunk)], axis=1)
                 for rho in range(r)], axis=0)
        xn = _rms(xr, g_ref[...]).astype(BF16)
        for c in range(3):
            col = (c * N_DIL + g) * D_MODEL
            y = _dot(xn, w_ref[:, col:col + D_MODEL])
            if c == 0:
                y = y * (DIL_HD ** -0.5)
            y = y.astype(BF16)
            for rho in range(r):
                o_refs[g][rho, :, c * D_MODEL:(c + 1) * D_MODEL] = y[rho * n:(rho + 1) * n]


def _dil_proj(x, g, w_qkv):
    t = x.shape[0]
    n = 3 * N_DIL * D_MODEL
    return pl.pallas_call(
        _dil_proj_kernel,
        grid=(t // TM,),
        in_specs=[pl.BlockSpec((TM, D_MODEL), lambda i: (i, 0)), _const_spec((1, D_MODEL)),
                  _const_spec((D_MODEL, n))],
        out_specs=[pl.BlockSpec((r, TM // r, 3 * D_MODEL), lambda i: (0, i, 0)) for _, r in DIL_GROUPS],
        out_shape=[jax.ShapeDtypeStruct((r, t // r, 3 * D_MODEL), BF16) for _, r in DIL_GROUPS],
        scratch_shapes=[pltpu.VMEM((D_MODEL // LANES, TM, LANES), F32)],
        compiler_params=_params(1),
        name="dil_proj",
    )(x, g, w_qkv)


def _t5_bucket(rel):
    half = NUM_BUCKETS // 2
    max_exact = half // 2
    ret = (rel > 0).astype(np.int32) * half
    n = np.abs(rel)
    large = max_exact + (np.log(np.maximum(n, 1) / max_exact) / np.log(MAX_DISTANCE / max_exact)
                         * (half - max_exact)).astype(np.int32)
    large = np.minimum(large, half - 1)
    return (ret + np.where(n < max_exact, n, large)).astype(np.int32)


def _dil_bias_table(rel_bias_cols, r):
    w, qb, hg = DIL_W, DIL_QB, DIL_HG
    rel = np.arange(qb + 2 * w)[None, :] - w - np.arange(qb)[:, None]
    band = np.abs(rel) <= w
    bucket = _t5_bucket(np.clip(rel, -w, w) * r).reshape(-1)
    onehot = np.zeros((NUM_BUCKETS, bucket.size), np.float32)
    onehot[bucket, np.arange(bucket.size)] = 1.0
    bias = lax.dot_general(rel_bias_cols.T, jnp.asarray(onehot), (((1,), (0,)), ((), ())),
                           precision=lax.Precision.HIGHEST).reshape(DIL_HEADS, qb, qb + 2 * w)
    bias = jnp.where(band[None], bias, NEG)
    return bias.reshape(DIL_HEADS // hg, hg * qb, qb + 2 * w)


def _dil_attn_kernel(bias_ref, q_ref, kp_ref, km_ref, kn_ref, vp_ref, vm_ref, vn_ref,
                     o_ref, lse_ref, kx_ref, vx_ref, *, sub_bounds):
    tq, w, qb, hg = DIL_TQ, DIL_W, DIL_QB, DIL_HG
    gl = hg * DIL_HD
    i = pl.program_id(1)
    kx_ref[0:w] = kp_ref[...]
    kx_ref[w:w + tq] = km_ref[...]
    kx_ref[w + tq:] = kn_ref[...]
    vx_ref[0:w] = vp_ref[...]
    vx_ref[w:w + tq] = vm_ref[...]
    vx_ref[w + tq:] = vn_ref[...]
    start = i * tq
    lo = jnp.int32(0)
    hi = jnp.int32(sub_bounds[1])
    for b0, b1 in zip(sub_bounds[1:-1], sub_bounds[2:]):
        inside = start >= b0
        lo = jnp.where(inside, b0, lo)
        hi = jnp.where(inside, b1, hi)
    key_iota = lax.broadcasted_iota(jnp.int32, (1, qb + 2 * w), 1)
    head_of_lane = lax.broadcasted_iota(jnp.int32, (qb, gl), 1) // DIL_HD
    lane = lax.broadcasted_iota(jnp.int32, (qb, LANES), 1)

    for jb in range(tq // qb):
        rows = slice(jb * qb, (jb + 1) * qb)
        krows = slice(jb * qb, (jb + 1) * qb + 2 * w)
        kpos = start + jb * qb - w + key_iota
        edge = jnp.where((kpos >= lo) & (kpos < hi), 0.0, NEG)
        lse_acc = jnp.zeros((qb, LANES), F32)
        for p in range(DIL_HEADS // hg):
            sl = slice(p * gl, (p + 1) * gl)
            qg = q_ref[rows, sl]
            zero = jnp.zeros_like(qg)
            qs = jnp.concatenate([jnp.where(head_of_lane == h, qg, zero) for h in range(hg)], axis=0)
            s = _dot_nt(qs, kx_ref[krows, sl]) + bias_ref[p] + edge
            mx = jnp.max(s, axis=-1, keepdims=True)
            pr = jnp.exp(s - mx)
            l = jnp.sum(pr, axis=-1, keepdims=True)
            pv = _dot(pr.astype(BF16), vx_ref[krows, sl]) / l
            lse = mx + jnp.log(l)
            og = pv[:qb]
            for h in range(hg):
                if h:
                    og = jnp.where(head_of_lane == h, pv[h * qb:(h + 1) * qb], og)
                lse_acc = jnp.where(lane == p * hg + h, lse[h * qb:(h + 1) * qb], lse_acc)
            o_ref[rows, sl] = og
        lse_ref[rows, :] = lse_acc


def _dil_attn(qkv, bias, r, seq_bounds):
    rows = qkv.shape[1]
    tq, w = DIL_TQ, DIL_W
    n_blk = rows // tq
    n_w = rows // w
    per = tq // w
    sub_bounds = tuple(b // r for b in seq_bounds)
    assert all(b % tq == 0 for b in sub_bounds)

    def main(col):
        return pl.BlockSpec((None, tq, D_MODEL), lambda rho, i: (rho, i, col))

    def prev(col):
        return pl.BlockSpec((None, w, D_MODEL), lambda rho, i: (rho, jnp.maximum(i * per - 1, 0), col))

    def nxt(col):
        return pl.BlockSpec((None, w, D_MODEL), lambda rho, i: (rho, jnp.minimum((i + 1) * per, n_w - 1), col))

    return pl.pallas_call(
        functools.partial(_dil_attn_kernel, sub_bounds=sub_bounds),
        grid=(r, n_blk),
        in_specs=[_const_spec(bias.shape), main(0), prev(1), main(1), nxt(1), prev(2), main(2), nxt(2)],
        out_specs=[pl.BlockSpec((None, tq, D_MODEL), lambda rho, i: (rho, i, 0)),
                   pl.BlockSpec((None, tq, LANES), lambda rho, i: (rho, i, 0))],
        out_shape=[jax.ShapeDtypeStruct((r, rows, D_MODEL), F32),
                   jax.ShapeDtypeStruct((r, rows, LANES), F32)],
        scratch_shapes=[pltpu.VMEM((tq + 2 * w, D_MODEL), BF16)] * 2,
        compiler_params=_params(2),
        name=f"dil_attn_r{r}",
    )(bias, qkv, qkv, qkv, qkv, qkv, qkv, qkv)


def _dil_out_kernel(x_ref, o0_ref, o1_ref, o2_ref, l0_ref, l1_ref, l2_ref, ex_ref, w_ref, out_ref,
                    os_ref, ls_ref):
    def token_order(src_ref, dst_ref, r):
        if r == 1:
            return src_ref[0]
        n_chunk = src_ref.shape[-1] // LANES
        for rho in range(r):
            for j in range(n_chunk):
                dst_ref[j, pl.ds(rho, TM // r, stride=r), :] = src_ref[rho, :, j * LANES:(j + 1) * LANES]
        return jnp.concatenate([dst_ref[j] for j in range(n_chunk)], axis=1)

    rs = [r for _, r in DIL_GROUPS]
    lses = [token_order(l_ref, ls_ref.at[g], r) for g, (l_ref, r) in enumerate(zip((l0_ref, l1_ref, l2_ref), rs))]
    mx = jnp.maximum(jnp.maximum(lses[0], lses[1]), lses[2])
    es = [jnp.exp(l - mx) for l in lses]
    inv = 1.0 / (es[0] + es[1] + es[2])
    ex = ex_ref[...]
    acc = None
    for g, (e, o_ref, r) in enumerate(zip(es, (o0_ref, o1_ref, o2_ref), rs)):
        wt = e * inv
        hi = wt.astype(BF16)
        lo = (wt - hi.astype(F32)).astype(BF16)
        term = (_dot(hi, ex) + _dot(lo, ex)) * token_order(o_ref, os_ref, r)
        acc = term if acc is None else acc + term
    out_ref[...] = x_ref[...] + _dot(acc.astype(BF16), w_ref[...])


def _dil_out(x, outs, lses, w_out):
    t = x.shape[0]
    expand = np.zeros((LANES, D_MODEL), np.float32)
    for h in range(DIL_HEADS):
        expand[h, h * DIL_HD:(h + 1) * DIL_HD] = 1.0
    row = pl.BlockSpec((TM, D_MODEL), lambda i: (i, 0))
    res = lambda n: [pl.BlockSpec((r, TM // r, n), lambda i: (0, i, 0)) for _, r in DIL_GROUPS]
    return pl.pallas_call(
        _dil_out_kernel,
        grid=(t // TM,),
        in_specs=[row] + res(D_MODEL) + res(LANES) + [_const_spec((LANES, D_MODEL)), _const_spec((D_MODEL, D_MODEL))],
        out_specs=row,
        out_shape=jax.ShapeDtypeStruct((t, D_MODEL), F32),
        scratch_shapes=[pltpu.VMEM((D_MODEL // LANES, TM, LANES), F32), pltpu.VMEM((N_DIL, 1, TM, LANES), F32)],
        compiler_params=_params(1),
        name="dil_out",
    )(x, *outs, *lses, jnp.asarray(expand, BF16), w_out)


def kernel(x_prompt, x_sample, mem_prompt, mem_sample, rel_bias, norm_ffn1, ffn1_in, ffn1_out, norm_mix,
           gla_w_in, gla_wg_f, gla_bg_f, gla_wg_b, gla_bg_b, gla_norm, gla_w_out, dil_w_qkv, dil_w_out,
           norm_cross, norm_mem, cross_w_q, cross_w_kv, cross_w_o, norm_ffn2, ffn2_in, ffn2_out, norm_final):
    bp, sp, _ = x_prompt.shape
    bs, ss, _ = x_sample.shape
    depth = norm_ffn1.shape[0]
    x = jnp.concatenate([x_prompt.reshape(bp * sp, D_MODEL), x_sample.reshape(bs * ss, D_MODEL)], axis=0)
    mem = jnp.concatenate([mem_prompt, mem_sample], axis=0)
    seq_bounds = tuple(i * sp for i in range(bp)) + tuple(bp * sp + i * ss for i in range(bs + 1))
    row = lambda v: v.reshape(1, -1)
    bf = lambda v: v.astype(BF16)
    dq = GLA_HEADS * GLA_DK
    dv = GLA_HEADS * GLA_DV

    for i in range(depth):
        x = _ffn(x, row(norm_ffn1[i]), bf(ffn1_in[i]), bf(ffn1_out[i]), row(norm_final), False)
        j = i // 2
        if i % 2 == 0:
            w_in = gla_w_in[j]
            zero = jnp.zeros((GLA_RANK, dq), F32)
            w_g = jnp.concatenate([jnp.concatenate([gla_wg_f[j], zero], axis=1),
                                   jnp.concatenate([zero, gla_wg_b[j]], axis=1)], axis=0)
            b_g = jnp.concatenate([gla_bg_f[j], gla_bg_b[j]]).reshape(1, 2 * dq)
            q, k, v, gate, gf, gb = _gla_proj(x, row(norm_mix[i]), bf(w_in[:, :2 * dq + 2 * dv]),
                                              bf(w_in[:, 2 * dq + 2 * dv:]), bf(w_g), b_g)
            o_f, o_b = _gla_scan(q, k, v, gf, gb, seq_bounds)
            x = _gla_out(x, o_f, o_b, gate, row(gla_norm[j]), bf(gla_w_out[j]))
        else:
            qkvs = _dil_proj(x, row(norm_mix[i]), bf(dil_w_qkv[j]))
            outs, lses = [], []
            for g, (_, r) in enumerate(DIL_GROUPS):
                bias = _dil_bias_table(rel_bias[:, g * DIL_HEADS:(g + 1) * DIL_HEADS], r)
                o, lse = _dil_attn(qkvs[g], bias, r, seq_bounds)
                outs.append(o)
                lses.append(lse)
            x = _dil_out(x, outs, lses, bf(dil_w_out[j]))
        k_mem, v_mem = _cross_kv(mem, row(norm_mem[i]), bf(cross_w_kv[i]))
        x = _cross(x, row(norm_cross[i]), bf(cross_w_q[i]), k_mem, v_mem, bf(cross_w_o[i]), seq_bounds)
        x = _ffn(x, row(norm_ffn2[i]), bf(ffn2_in[i]), bf(ffn2_out[i]), row(norm_final), i == depth - 1)

    y_prompt = x[:bp * sp].reshape(bp, sp, D_MODEL)
    y_sample = x[bp * sp:].reshape(bs, ss, D_MODEL)
    return (y_prompt, y_sample)
```

```python
import functools

import numpy as np
import jax
import jax.numpy as jnp
from jax import lax
from jax.experimental import pallas as pl
from jax.experimental.pallas import tpu as pltpu

F32 = jnp.float32
BF16 = jnp.bfloat16

D_MODEL = 1024
D_FF = 2816
EPS = 1e-6
MEM_LEN = 256
GLA_HEADS = 4
GLA_DK = 128
GLA_DV = 256
GLA_RANK = 16
GLA_TAU = 16.0
DIL_GROUPS = ((128, 1), (512, 4), (2048, 16))
N_DIL = len(DIL_GROUPS)
DIL_HEADS = 16
DIL_HD = 64
DIL_W = 64
NUM_BUCKETS = 32
MAX_DISTANCE = 1024
X_HEADS = 4
X_HD = 256
NEG = -1e30
LOG2E = 1.4426950408889634

LANES = 128
VMEM_LIMIT = 56 * 1024 * 1024

TM = 256
GLA_C = 256
DIL_TQ = 256
DIL_QB = 128
DIL_HG = 4


def _params(n_axes, vmem=VMEM_LIMIT):
    return pltpu.CompilerParams(dimension_semantics=("arbitrary",) * n_axes, vmem_limit_bytes=vmem)


def _const_spec(shape):
    nd = len(shape)
    return pl.BlockSpec(shape, lambda *_: (0,) * nd, pipeline_mode=pl.Buffered(1))


def _rms(x, g):
    return x * lax.rsqrt(jnp.mean(x * x, axis=-1, keepdims=True) + EPS) * g


def _dot(a, b):
    return jnp.dot(a, b, preferred_element_type=F32)


def _dot_nt(a, b):
    return lax.dot_general(a, b, (((1,), (1,)), ((), ())), preferred_element_type=F32)


def _dot_tn(a, b):
    return lax.dot_general(a, b, (((0,), (0,)), ((), ())), preferred_element_type=F32)


def _ffn_kernel(x_ref, g_ref, win_ref, wout_ref, gfin_ref, o_ref, *, final_norm):
    x = x_ref[...]
    xn = _rms(x, g_ref[...]).astype(BF16)
    a = _dot(xn, win_ref[:, :D_FF])
    b = _dot(xn, win_ref[:, D_FF:])
    h = (a * jax.nn.sigmoid(a) * b).astype(BF16)
    y = x + 0.5 * _dot(h, wout_ref[...])
    if final_norm:
        y = _rms(y, gfin_ref[...])
    o_ref[...] = y


def _ffn(x, g, w_in, w_out, g_final, final_norm):
    t = x.shape[0]
    return pl.pallas_call(
        functools.partial(_ffn_kernel, final_norm=final_norm),
        grid=(t // TM,),
        in_specs=[
            pl.BlockSpec((TM, D_MODEL), lambda i: (i, 0)),
            _const_spec((1, D_MODEL)),
            _const_spec((D_MODEL, 2 * D_FF)),
            _const_spec((D_FF, D_MODEL)),
            _const_spec((1, D_MODEL)),
        ],
        out_specs=pl.BlockSpec((TM, D_MODEL), lambda i: (i, 0)),
        out_shape=jax.ShapeDtypeStruct((t, D_MODEL), F32),
        compiler_params=_params(1),
        name="ffn",
    )(x, g, w_in, w_out, g_final)


def _kv_kernel(m_ref, g_ref, w_ref, k_ref, v_ref):
    mn = _rms(m_ref[0], g_ref[...]).astype(BF16)
    kv = _dot(mn, w_ref[...])
    k_ref[0] = kv[:, :D_MODEL].astype(BF16)
    v_ref[0] = kv[:, D_MODEL:].astype(BF16)


def _cross_kv(mem, g, w_kv):
    nb = mem.shape[0]
    return pl.pallas_call(
        _kv_kernel,
        grid=(nb,),
        in_specs=[
            pl.BlockSpec((1, MEM_LEN, D_MODEL), lambda i: (i, 0, 0)),
            _const_spec((1, D_MODEL)),
            _const_spec((D_MODEL, 2 * D_MODEL)),
        ],
        out_specs=[pl.BlockSpec((1, MEM_LEN, D_MODEL), lambda i: (i, 0, 0))] * 2,
        out_shape=[jax.ShapeDtypeStruct((nb, MEM_LEN, D_MODEL), BF16)] * 2,
        compiler_params=_params(1),
        name="cross_kv",
    )(mem, g, w_kv)


def _cross_kernel(x_ref, g_ref, wq_ref, k_ref, v_ref, wo_ref, o_ref):
    x = x_ref[...]
    xn = _rms(x, g_ref[...]).astype(BF16)
    q = (_dot(xn, wq_ref[...]) * (X_HD ** -0.5)).astype(BF16)
    outs = []
    for h in range(X_HEADS):
        sl = slice(h * X_HD, (h + 1) * X_HD)
        s = _dot_nt(q[:, sl], k_ref[0, :, sl])
        p = jnp.exp(s - jnp.max(s, axis=-1, keepdims=True))
        l = jnp.sum(p, axis=-1, keepdims=True)
        outs.append(_dot(p.astype(BF16), v_ref[0, :, sl]) / l)
    o = jnp.concatenate(outs, axis=-1).astype(BF16)
    o_ref[...] = x + _dot(o, wo_ref[...])


def _cross(x, g, w_q, k, v, w_o, seq_bounds):
    t = x.shape[0]

    def mem_idx(i):
        tok = i * TM
        idx = 0
        for b in seq_bounds[1:-1]:
            idx = idx + (tok >= b).astype(jnp.int32)
        return (idx, 0, 0)

    return pl.pallas_call(
        _cross_kernel,
        grid=(t // TM,),
        in_specs=[
            pl.BlockSpec((TM, D_MODEL), lambda i: (i, 0)),
            _const_spec((1, D_MODEL)),
            _const_spec((D_MODEL, D_MODEL)),
            pl.BlockSpec((1, MEM_LEN, D_MODEL), mem_idx),
            pl.BlockSpec((1, MEM_LEN, D_MODEL), mem_idx),
            _const_spec((D_MODEL, D_MODEL)),
        ],
        out_specs=pl.BlockSpec((TM, D_MODEL), lambda i: (i, 0)),
        out_shape=jax.ShapeDtypeStruct((t, D_MODEL), F32),
        compiler_params=_params(1),
        name="cross_attn",
    )(x, g, w_q, k, v, w_o)


def _log_sigmoid(x):
    return jnp.minimum(x, 0.0) - jnp.log1p(jnp.exp(-jnp.abs(x)))


def _gla_proj_kernel(x_ref, g_ref, w_ref, wz_ref, wg_ref, bg_ref,
                     q_ref, k_ref, v_ref, gate_ref, gf_ref, gb_ref):
    dq = GLA_HEADS * GLA_DK
    dv = GLA_HEADS * GLA_DV
    xn = _rms(x_ref[...], g_ref[...]).astype(BF16)
    q_ref[...] = _dot(xn, w_ref[:, :dq]) * (GLA_DK ** -0.5)
    k_ref[...] = _dot(xn, w_ref[:, dq:2 * dq])
    v_ref[...] = _dot(xn, w_ref[:, 2 * dq:2 * dq + dv]).astype(BF16)
    r = _dot(xn, w_ref[:, 2 * dq + dv:])
    gate_ref[...] = r * jax.nn.sigmoid(r)
    z = _dot(xn, wz_ref[...]).astype(BF16)
    pre = _dot(z, wg_ref[...]) + bg_ref[...]
    lg = _log_sigmoid(pre) * (1.0 / GLA_TAU)
    gf_ref[...] = lg[:, :dq]
    gb_ref[...] = lg[:, dq:]


def _gla_proj(x, g, w_main, w_z, w_g, b_g):
    t = x.shape[0]
    dq = GLA_HEADS * GLA_DK
    dv = GLA_HEADS * GLA_DV
    row = lambda n: pl.BlockSpec((TM, n), lambda i: (i, 0))
    return pl.pallas_call(
        _gla_proj_kernel,
        grid=(t // TM,),
        in_specs=[
            row(D_MODEL),
            _const_spec((1, D_MODEL)),
            _const_spec((D_MODEL, 2 * dq + 2 * dv)),
            _const_spec((D_MODEL, 2 * GLA_RANK)),
            _const_spec((2 * GLA_RANK, 2 * dq)),
            _const_spec((1, 2 * dq)),
        ],
        out_specs=[row(dq), row(dq), row(dv), row(dv), row(dq), row(dq)],
        out_shape=[
            jax.ShapeDtypeStruct((t, dq), F32),
            jax.ShapeDtypeStruct((t, dq), F32),
            jax.ShapeDtypeStruct((t, dv), BF16),
            jax.ShapeDtypeStruct((t, dv), F32),
            jax.ShapeDtypeStruct((t, dq), F32),
            jax.ShapeDtypeStruct((t, dq), F32),
        ],
        compiler_params=_params(1),
        name="gla_proj",
    )(x, g, w_main, w_z, w_g, b_g)


def _gla_tables(c):
    n_lev = c.bit_length() - 1
    half = c // 2
    idx = np.arange(c)
    tri = (idx[None, :] <= idx[:, None]).astype(np.float32)
    lvt = np.full((half, half), -2, np.int32)
    lvt[idx[:half], idx[:half]] = -1
    for i in range(half):
        for j in range(i):
            lvt[i, j] = n_lev - (i ^ j).bit_length()
    sgn = np.stack([np.where((idx // (c >> (l + 1))) % 2 == 1, 1.0, -1.0) for l in range(n_lev)])
    tri_all = np.stack([tri, tri[::-1, ::-1]])
    lvt_all = np.stack([lvt, lvt[::-1, ::-1]])
    sgn_all = np.broadcast_to(np.stack([sgn, sgn[:, ::-1]])[..., None], (2, n_lev, c, LANES))
    rm4 = np.broadcast_to((idx % 4)[:, None], (c, LANES))
    return (jnp.asarray(tri_all, BF16), jnp.asarray(sgn_all, F32), jnp.asarray(lvt_all, jnp.int32),
            jnp.asarray(rm4, jnp.int32))


def _stack_pair(x, low):
    zero = jnp.zeros_like(x)
    return jnp.concatenate([jnp.where(low, x, zero), jnp.where(low, zero, x)], axis=0)


def _gla_ref_rows(b, m, off, rm4):
    c, n = b.shape
    span = 2 * m
    if span >= 8:
        return jnp.concatenate(
            [jnp.broadcast_to(b[s + off:s + off + 1, :], (span, n)) for s in range(0, c, span)], axis=0)
    assert span == 4
    ref = b
    for rho in range(span):
        if rho != off:
            hit = rm4 == rho
            rolled = pltpu.roll(b, (rho - off) % c, 0)
            ref = jnp.concatenate([jnp.where(hit, rolled[:, j:j + LANES], ref[:, j:j + LANES])
                                   for j in range(0, n, LANES)], axis=1)
    return ref


def _gla_scan_kernel(tri_ref, sgn_ref, lvt_ref, rm4_ref,
                     qf_ref, kf_ref, vf_ref, gf_ref,
                     qb_ref, kb_ref, vb_ref, gb_ref,
                     of_ref, ob_ref, state_ref, *, resets_f, resets_b):
    c = GLA_C
    half = c // 2
    n_lev = c.bit_length() - 1
    dk, dv, dk2 = GLA_DK, GLA_DV, 2 * GLA_DK
    step = pl.program_id(0)

    def is_any(vals):
        cond = step == vals[0]
        for v in vals[1:]:
            cond = jnp.logical_or(cond, step == v)
        return cond

    @pl.when(is_any(resets_f))
    def _():
        state_ref[0] = jnp.zeros(state_ref.shape[1:], F32)

    @pl.when(is_any(resets_b))
    def _():
        state_ref[1] = jnp.zeros(state_ref.shape[1:], F32)

    low = lax.broadcasted_iota(jnp.int32, (c, dk2), 1) < dk
    low_half = lax.broadcasted_iota(jnp.int32, (half, dk2), 1) < dk
    rm4 = rm4_ref[...]
    heads = [slice(h * dk, (h + 1) * dk) for h in range(GLA_HEADS)]

    dirs = ((0, qf_ref, kf_ref, vf_ref, gf_ref, of_ref, c - 1, slice(half, c), slice(0, half)),
            (1, qb_ref, kb_ref, vb_ref, gb_ref, ob_ref, 0, slice(0, half), slice(half, c)))
    for d, q_ref, k_ref, v_ref, g_ref, o_ref, last_row, late_rows, early_rows in dirs:
        q = q_ref[...]
        k = k_ref[...]
        g = g_ref[...]
        n = q.shape[1]
        g1 = g.astype(BF16)
        r1 = g - g1.astype(F32)
        g2 = r1.astype(BF16)
        g3 = (r1 - g2.astype(F32)).astype(BF16)
        bb = _dot(tri_ref[d], jnp.concatenate([g1, g2, g3], axis=1))
        b2 = (bb[:, :n] + bb[:, n:2 * n] + bb[:, 2 * n:]) * LOG2E
        eb = jnp.exp2(b2)
        e_tot = eb[last_row:last_row + 1, :]
        qd = (q * eb).astype(BF16)
        kd = (k * jnp.exp2(b2[last_row:last_row + 1, :] - b2)).astype(BF16)
        eg = jnp.exp(g)
        lvt = lvt_ref[d]
        qk = q * k
        on_diag = lvt == -1
        dg = [[jnp.where(on_diag, jnp.sum(qk[t * half:(t + 1) * half, hs], axis=-1, keepdims=True), 0.0)
               for t in range(2)] for hs in heads]
        off = [None] * GLA_HEADS
        for l in range(n_lev):
            m = c >> (l + 1)
            s = sgn_ref[d, l]
            later = s > 0.0
            if m >= 2:
                dlt = b2 - _gla_ref_rows(b2, m, m - 1 if d == 0 else m, rm4)
                xs = [jnp.exp2(dlt[:, hs] * s) for hs in heads]
            else:
                xs = [jnp.where(later, eg[:, hs], 1.0) for hs in heads]
            w = jnp.concatenate([jnp.where(later, q[:, hs], k[:, hs]) * x for hs, x in zip(heads, xs)],
                                axis=1).astype(BF16)
            hit = lvt == l
            for p in range(GLA_HEADS // 2):
                wp = w[:, p * dk2:(p + 1) * dk2]
                if l == 0:
                    pm = _dot_nt(_stack_pair(wp[late_rows], low_half), wp[early_rows])
                    off[2 * p], off[2 * p + 1] = pm[:half], pm[half:]
                    continue
                pm = _dot_nt(_stack_pair(wp, low), wp)
                for hh in range(2):
                    for t in range(2):
                        tile = pm[hh * c + t * half:hh * c + (t + 1) * half, t * half:(t + 1) * half]
                        dg[2 * p + hh][t] = jnp.where(hit, tile, dg[2 * p + hh][t])
        zero_tile = jnp.zeros((half, half), BF16)
        for p in range(GLA_HEADS // 2):
            st = state_ref[d, p]
            inter = _dot_nt(_stack_pair(qd[:, p * dk2:(p + 1) * dk2], low), st.astype(BF16))
            vs = [v_ref[:, (2 * p + hh) * dv:(2 * p + hh + 1) * dv] for hh in range(2)]
            for hh in range(2):
                h = 2 * p + hh
                tiles = [[dg[h][0].astype(BF16), zero_tile], [zero_tile, dg[h][1].astype(BF16)]]
                if d == 0:
                    tiles[1][0] = off[h].astype(BF16)
                else:
                    tiles[0][1] = off[h].astype(BF16)
                a = jnp.concatenate([jnp.concatenate(r, axis=1) for r in tiles], axis=0)
                o_ref[:, h * dv:(h + 1) * dv] = inter[hh * c:(hh + 1) * c] + _dot(a, vs[hh])
            upd = _dot_tn(jnp.concatenate(vs, axis=0), _stack_pair(kd[:, p * dk2:(p + 1) * dk2], low))
            state_ref[d, p] = st * e_tot[:, p * dk2:(p + 1) * dk2] + upd


def _gla_scan(q, k, v, gf, gb, seq_bounds):
    t = q.shape[0]
    c = GLA_C
    n_chunks = t // c
    dq = GLA_HEADS * GLA_DK
    dv = GLA_HEADS * GLA_DV
    assert all(b % c == 0 for b in seq_bounds)
    assert c == 2 * LANES
    tables = _gla_tables(c)
    resets_f = tuple(b // c for b in seq_bounds[:-1])
    resets_b = tuple(n_chunks - b // c for b in seq_bounds[1:])
    fwd = lambda n: pl.BlockSpec((c, n), lambda i: (i, 0))
    bwd = lambda n: pl.BlockSpec((c, n), lambda i: (n_chunks - 1 - i, 0))
    return pl.pallas_call(
        functools.partial(_gla_scan_kernel, resets_f=resets_f, resets_b=resets_b),
        grid=(n_chunks,),
        in_specs=[_const_spec(tbl.shape) for tbl in tables] + [
            fwd(dq), fwd(dq), fwd(dv), fwd(dq),
            bwd(dq), bwd(dq), bwd(dv), bwd(dq),
        ],
        out_specs=[fwd(dv), bwd(dv)],
        out_shape=[jax.ShapeDtypeStruct((t, dv), F32)] * 2,
        scratch_shapes=[pltpu.VMEM((2, GLA_HEADS // 2, GLA_DV, 2 * GLA_DK), F32)],
        compiler_params=_params(1),
        name="gla_scan",
    )(*tables, q, k, v, gf, q, k, v, gb)


def _gla_out_kernel(x_ref, of_ref, ob_ref, gate_ref, gn_ref, w_ref, o_ref):
    o = of_ref[...] + ob_ref[...]
    outs = []
    for h in range(GLA_HEADS):
        sl = slice(h * GLA_DV, (h + 1) * GLA_DV)
        outs.append(_rms(o[:, sl], gn_ref[:, sl]))
    y = (jnp.concatenate(outs, axis=-1) * gate_ref[...]).astype(BF16)
    o_ref[...] = x_ref[...] + _dot(y, w_ref[...])


def _gla_out(x, o_f, o_b, gate, g_norm, w_out):
    t = x.shape[0]
    row = pl.BlockSpec((TM, D_MODEL), lambda i: (i, 0))
    return pl.pallas_call(
        _gla_out_kernel,
        grid=(t // TM,),
        in_specs=[row, row, row, row, _const_spec((1, D_MODEL)), _const_spec((D_MODEL, D_MODEL))],
        out_specs=row,
        out_shape=jax.ShapeDtypeStruct((t, D_MODEL), F32),
        compiler_params=_params(1),
        name="gla_out",
    )(x, o_f, o_b, gate, g_norm, w_out)


def _dil_proj_kernel(x_ref, g_ref, w_ref, o0_ref, o1_ref, o2_ref, xs_ref):
    o_refs = (o0_ref, o1_ref, o2_ref)
    n_chunk = D_MODEL // LANES
    for j in range(n_chunk):
        xs_ref[j] = x_ref[:, j * LANES:(j + 1) * LANES]
    for g, (_, r) in enumerate(DIL_GROUPS):
        n = TM // r
        if r == 1:
            xr = x_ref[...]
        else:
            xr = jnp.concatenate(
                [jnp.concatenate([xs_ref[j, pl.ds(rho, n, stride=r), :] for j in range(n_chunk)], axis=1)
                 for rho in range(r)], axis=0)
        xn = _rms(xr, g_ref[...]).astype(BF16)
        for c in range(3):
            col = (c * N_DIL + g) * D_MODEL
            y = _dot(xn, w_ref[:, col:col + D_MODEL])
            if c == 0:
                y = y * (DIL_HD ** -0.5)
            y = y.astype(BF16)
            for rho in range(r):
                o_refs[g][rho, :, c * D_MODEL:(c + 1) * D_MODEL] = y[rho * n:(rho + 1) * n]


def _dil_proj(x, g, w_qkv):
    t = x.shape[0]
    n = 3 * N_DIL * D_MODEL
    return pl.pallas_call(
        _dil_proj_kernel,
        grid=(t // TM,),
        in_specs=[pl.BlockSpec((TM, D_MODEL), lambda i: (i, 0)), _const_spec((1, D_MODEL)),
                  _const_spec((D_MODEL, n))],
        out_specs=[pl.BlockSpec((r, TM // r, 3 * D_MODEL), lambda i: (0, i, 0)) for _, r in DIL_GROUPS],
        out_shape=[jax.ShapeDtypeStruct((r, t // r, 3 * D_MODEL), BF16) for _, r in DIL_GROUPS],
        scratch_shapes=[pltpu.VMEM((D_MODEL // LANES, TM, LANES), F32)],
        compiler_params=_params(1),
        name="dil_proj",
    )(x, g, w_qkv)


def _t5_bucket(rel):
    half = NUM_BUCKETS // 2
    max_exact = half // 2
    ret = (rel > 0).astype(np.int32) * half
    n = np.abs(rel)
    large = max_exact + (np.log(np.maximum(n, 1) / max_exact) / np.log(MAX_DISTANCE / max_exact)
                         * (half - max_exact)).astype(np.int32)
    large = np.minimum(large, half - 1)
    return (ret + np.where(n < max_exact, n, large)).astype(np.int32)


def _dil_bias_table(rel_bias_cols, r):
    w, qb, hg = DIL_W, DIL_QB, DIL_HG
    rel = np.arange(qb + 2 * w)[None, :] - w - np.arange(qb)[:, None]
    band = np.abs(rel) <= w
    bucket = _t5_bucket(np.clip(rel, -w, w) * r).reshape(-1)
    onehot = np.zeros((NUM_BUCKETS, bucket.size), np.float32)
    onehot[bucket, np.arange(bucket.size)] = 1.0
    bias = lax.dot_general(rel_bias_cols.T, jnp.asarray(onehot), (((1,), (0,)), ((), ())),
                           precision=lax.Precision.HIGHEST).reshape(DIL_HEADS, qb, qb + 2 * w)
    bias = jnp.where(band[None], bias, NEG)
    return bias.reshape(DIL_HEADS // hg, hg * qb, qb + 2 * w)


def _dil_attn_kernel(bias_ref, q_ref, kp_ref, km_ref, kn_ref, vp_ref, vm_ref, vn_ref,
                     o_ref, lse_ref, kx_ref, vx_ref, *, sub_bounds):
    tq, w, qb, hg = DIL_TQ, DIL_W, DIL_QB, DIL_HG
    gl = hg * DIL_HD
    i = pl.program_id(1)
    kx_ref[0:w] = kp_ref[...]
    kx_ref[w:w + tq] = km_ref[...]
    kx_ref[w + tq:] = kn_ref[...]
    vx_ref[0:w] = vp_ref[...]
    vx_ref[w:w + tq] = vm_ref[...]
    vx_ref[w + tq:] = vn_ref[...]
    start = i * tq
    lo = jnp.int32(0)
    hi = jnp.int32(sub_bounds[1])
    for b0, b1 in zip(sub_bounds[1:-1], sub_bounds[2:]):
        inside = start >= b0
        lo = jnp.where(inside, b0, lo)
        hi = jnp.where(inside, b1, hi)
    key_iota = lax.broadcasted_iota(jnp.int32, (1, qb + 2 * w), 1)
    head_of_lane = lax.broadcasted_iota(jnp.int32, (qb, gl), 1) // DIL_HD
    lane = lax.broadcasted_iota(jnp.int32, (qb, LANES), 1)

    for jb in range(tq // qb):
        rows = slice(jb * qb, (jb + 1) * qb)
        krows = slice(jb * qb, (jb + 1) * qb + 2 * w)
        kpos = start + jb * qb - w + key_iota
        edge = jnp.where((kpos >= lo) & (kpos < hi), 0.0, NEG)
        lse_acc = jnp.zeros((qb, LANES), F32)
        for p in range(DIL_HEADS // hg):
            sl = slice(p * gl, (p + 1) * gl)
            qg = q_ref[rows, sl]
            zero = jnp.zeros_like(qg)
            qs = jnp.concatenate([jnp.where(head_of_lane == h, qg, zero) for h in range(hg)], axis=0)
            s = _dot_nt(qs, kx_ref[krows, sl]) + bias_ref[p] + edge
            mx = jnp.max(s, axis=-1, keepdims=True)
            pr = jnp.exp(s - mx)
            l = jnp.sum(pr, axis=-1, keepdims=True)
            pv = _dot(pr.astype(BF16), vx_ref[krows, sl]) / l
            lse = mx + jnp.log(l)
            og = pv[:qb]
            for h in range(hg):
                if h:
                    og = jnp.where(head_of_lane == h, pv[h * qb:(h + 1) * qb], og)
                lse_acc = jnp.where(lane == p * hg + h, lse[h * qb:(h + 1) * qb], lse_acc)
            o_ref[rows, sl] = og
        lse_ref[rows, :] = lse_acc


def _dil_attn(qkv, bias, r, seq_bounds):
    rows = qkv.shape[1]
    tq, w = DIL_TQ, DIL_W
    n_blk = rows // tq
    n_w = rows // w
    per = tq // w
    sub_bounds = tuple(b // r for b in seq_bounds)
    assert all(b % tq == 0 for b in sub_bounds)

    def main(col):
        return pl.BlockSpec((None, tq, D_MODEL), lambda rho, i: (rho, i, col))

    def prev(col):
        return pl.BlockSpec((None, w, D_MODEL), lambda rho, i: (rho, jnp.maximum(i * per - 1, 0), col))

    def nxt(col):
        return pl.BlockSpec((None, w, D_MODEL), lambda rho, i: (rho, jnp.minimum((i + 1) * per, n_w - 1), col))

    return pl.pallas_call(
        functools.partial(_dil_attn_kernel, sub_bounds=sub_bounds),
        grid=(r, n_blk),
        in_specs=[_const_spec(bias.shape), main(0), prev(1), main(1), nxt(1), prev(2), main(2), nxt(2)],
        out_specs=[pl.BlockSpec((None, tq, D_MODEL), lambda rho, i: (rho, i, 0)),
                   pl.BlockSpec((None, tq, LANES), lambda rho, i: (rho, i, 0))],
        out_shape=[jax.ShapeDtypeStruct((r, rows, D_MODEL), F32),
                   jax.ShapeDtypeStruct((r, rows, LANES), F32)],
        scratch_shapes=[pltpu.VMEM((tq + 2 * w, D_MODEL), BF16)] * 2,
        compiler_params=_params(2),
        name=f"dil_attn_r{r}",
    )(bias, qkv, qkv, qkv, qkv, qkv, qkv, qkv)


def _dil_out_kernel(x_ref, o0_ref, o1_ref, o2_ref, l0_ref, l1_ref, l2_ref, ex_ref, w_ref, out_ref,
                    os_ref, ls_ref):
    def token_order(src_ref, dst_ref, r):
        if r == 1:
            return src_ref[0]
        n_chunk = src_ref.shape[-1] // LANES
        for rho in range(r):
            for j in range(n_chunk):
                dst_ref[j, pl.ds(rho, TM // r, stride=r), :] = src_ref[rho, :, j * LANES:(j + 1) * LANES]
        return jnp.concatenate([dst_ref[j] for j in range(n_chunk)], axis=1)

    rs = [r for _, r in DIL_GROUPS]
    lses = [token_order(l_ref, ls_ref.at[g], r) for g, (l_ref, r) in enumerate(zip((l0_ref, l1_ref, l2_ref), rs))]
    mx = jnp.maximum(jnp.maximum(lses[0], lses[1]), lses[2])
    es = [jnp.exp(l - mx) for l in lses]
    inv = 1.0 / (es[0] + es[1] + es[2])
    ex = ex_ref[...]
    acc = None
    for g, (e, o_ref, r) in enumerate(zip(es, (o0_ref, o1_ref, o2_ref), rs)):
        wt = e * inv
        hi = wt.astype(BF16)
        lo = (wt - hi.astype(F32)).astype(BF16)
        term = (_dot(hi, ex) + _dot(lo, ex)) * token_order(o_ref, os_ref, r)
        acc = term if acc is None else acc + term
    out_ref[...] = x_ref[...] + _dot(acc.astype(BF16), w_ref[...])


def _dil_out(x, outs, lses, w_out):
    t = x.shape[0]
    expand = np.zeros((LANES, D_MODEL), np.float32)
    for h in range(DIL_HEADS):
        expand[h, h * DIL_HD:(h + 1) * DIL_HD] = 1.0
    row = pl.BlockSpec((TM, D_MODEL), lambda i: (i, 0))
    res = lambda n: [pl.BlockSpec((r, TM // r, n), lambda i: (0, i, 0)) for _, r in DIL_GROUPS]
    return pl.pallas_call(
        _dil_out_kernel,
        grid=(t // TM,),
        in_specs=[row] + res(D_MODEL) + res(LANES) + [_const_spec((LANES, D_MODEL)), _const_spec((D_MODEL, D_MODEL))],
        out_specs=row,
        out_shape=jax.ShapeDtypeStruct((t, D_MODEL), F32),
        scratch_shapes=[pltpu.VMEM((D_MODEL // LANES, TM, LANES), F32), pltpu.VMEM((N_DIL, 1, TM, LANES), F32)],
        compiler_params=_params(1),
        name="dil_out",
    )(x, *outs, *lses, jnp.asarray(expand, BF16), w_out)


def kernel(x_prompt, x_sample, mem_prompt, mem_sample, rel_bias, norm_ffn1, ffn1_in, ffn1_out, norm_mix,
           gla_w_in, gla_wg_f, gla_bg_f, gla_wg_b, gla_bg_b, gla_norm, gla_w_out, dil_w_qkv, dil_w_out,
           norm_cross, norm_mem, cross_w_q, cross_w_kv, cross_w_o, norm_ffn2, ffn2_in, ffn2_out, norm_final):
    bp, sp, _ = x_prompt.shape
    bs, ss, _ = x_sample.shape
    depth = norm_ffn1.shape[0]
    x = jnp.concatenate([x_prompt.reshape(bp * sp, D_MODEL), x_sample.reshape(bs * ss, D_MODEL)], axis=0)
    mem = jnp.concatenate([mem_prompt, mem_sample], axis=0)
    seq_bounds = tuple(i * sp for i in range(bp)) + tuple(bp * sp + i * ss for i in range(bs + 1))
    row = lambda v: v.reshape(1, -1)
    bf = lambda v: v.astype(BF16)
    dq = GLA_HEADS * GLA_DK
    dv = GLA_HEADS * GLA_DV

    for i in range(depth):
        x = _ffn(x, row(norm_ffn1[i]), bf(ffn1_in[i]), bf(ffn1_out[i]), row(norm_final), False)
        j = i // 2
        if i % 2 == 0:
            w_in = gla_w_in[j]
            zero = jnp.zeros((GLA_RANK, dq), F32)
            w_g = jnp.concatenate([jnp.concatenate([gla_wg_f[j], zero], axis=1),
                                   jnp.concatenate([zero, gla_wg_b[j]], axis=1)], axis=0)
            b_g = jnp.concatenate([gla_bg_f[j], gla_bg_b[j]]).reshape(1, 2 * dq)
            q, k, v, gate, gf, gb = _gla_proj(x, row(norm_mix[i]), bf(w_in[:, :2 * dq + 2 * dv]),
                                              bf(w_in[:, 2 * dq + 2 * dv:]), bf(w_g), b_g)
            o_f, o_b = _gla_scan(q, k, v, gf, gb, seq_bounds)
            x = _gla_out(x, o_f, o_b, gate, row(gla_norm[j]), bf(gla_w_out[j]))
        else:
            qkvs = _dil_proj(x, row(norm_mix[i]), bf(dil_w_qkv[j]))
            outs, lses = [], []
            for g, (_, r) in enumerate(DIL_GROUPS):
                bias = _dil_bias_table(rel_bias[:, g * DIL_HEADS:(g + 1) * DIL_HEADS], r)
                o, lse = _dil_attn(qkvs[g], bias, r, seq_bounds)
                outs.append(o)
                lses.append(lse)
            x = _dil_out(x, outs, lses, bf(dil_w_out[j]))
        k_mem, v_mem = _cross_kv(mem, row(norm_mem[i]), bf(cross_w_kv[i]))
        x = _cross(x, row(norm_cross[i]), bf(cross_w_q[i]), k_mem, v_mem, bf(cross_w_o[i]), seq_bounds)
        x = _ffn(x, row(norm_ffn2[i]), bf(ffn2_in[i]), bf(ffn2_out[i]), row(norm_final), i == depth - 1)

    y_prompt = x[:bp * sp].reshape(bp, sp, D_MODEL)
    y_sample = x[bp * sp:].reshape(bs, ss, D_MODEL)
    return (y_prompt, y_sample)
```

```python
import functools

import numpy as np
import jax
import jax.numpy as jnp
from jax import lax
from jax.experimental import pallas as pl
from jax.experimental.pallas import tpu as pltpu

F32 = jnp.float32
BF16 = jnp.bfloat16

D_MODEL = 1024
D_FF = 2816
EPS = 1e-6
MEM_LEN = 256
GLA_HEADS = 4
GLA_DK = 128
GLA_DV = 256
GLA_RANK = 16
GLA_TAU = 16.0
DIL_GROUPS = ((128, 1), (512, 4), (2048, 16))
N_DIL = len(DIL_GROUPS)
DIL_HEADS = 16
DIL_HD = 64
DIL_W = 64
NUM_BUCKETS = 32
MAX_DISTANCE = 1024
X_HEADS = 4
X_HD = 256
NEG = -1e30
LOG2E = 1.4426950408889634

LANES = 128
VMEM_LIMIT = 56 * 1024 * 1024

TM_FFN = 512
TM_PROJ = 512
TM_DIL = 256
TM_OUT = 512
GLA_C = 256
DIL_TQ = 512
DIL_QB = 128
DIL_HG = 4


def _params(n_axes, vmem=VMEM_LIMIT):
    return pltpu.CompilerParams(dimension_semantics=("arbitrary",) * n_axes, vmem_limit_bytes=vmem)


def _const_spec(shape):
    nd = len(shape)
    return pl.BlockSpec(shape, lambda *_: (0,) * nd, pipeline_mode=pl.Buffered(1))


def _rms(x, g):
    return x * lax.rsqrt(jnp.mean(x * x, axis=-1, keepdims=True) + EPS) * g


def _dot(a, b):
    return jnp.dot(a, b, preferred_element_type=F32)


def _dot_nt(a, b):
    return lax.dot_general(a, b, (((1,), (1,)), ((), ())), preferred_element_type=F32)


def _dot_tn(a, b):
    return lax.dot_general(a, b, (((0,), (0,)), ((), ())), preferred_element_type=F32)


def _ffn_kernel(*refs, n_in, n_out, n_first, final_norm):
    x_refs = refs[:n_in]
    g_ref, win_ref, wout_ref, gfin_ref = refs[n_in:n_in + 4]
    o_refs = refs[n_in + 4:]
    i = pl.program_id(0)
    x = x_refs[0][...] if n_in == 1 else jnp.where(i < n_first, x_refs[0][...], x_refs[1][...])
    xn = _rms(x, g_ref[...]).astype(BF16)
    a = _dot(xn, win_ref[:, :D_FF])
    b = _dot(xn, win_ref[:, D_FF:])
    h = (a * jax.nn.sigmoid(a) * b).astype(BF16)
    y = x + 0.5 * _dot(h, wout_ref[...])
    if final_norm:
        y = _rms(y, gfin_ref[...])
    if n_out == 1:
        o_refs[0][...] = y
    else:
        @pl.when(i < n_first)
        def _():
            o_refs[0][...] = y

        @pl.when(i >= n_first)
        def _():
            o_refs[1][...] = y


def _split_specs(n_first):
    return [pl.BlockSpec((TM_FFN, D_MODEL), lambda i: (jnp.minimum(i, n_first - 1), 0)),
            pl.BlockSpec((TM_FFN, D_MODEL), lambda i: (jnp.maximum(i - n_first, 0), 0))]


def _ffn(xs, g, w_in, w_out, g_final, final_norm, out_rows):
    t = sum(x.shape[0] for x in xs)
    n_first = (xs[0].shape[0] if len(xs) == 2 else out_rows[0]) // TM_FFN
    one = [pl.BlockSpec((TM_FFN, D_MODEL), lambda i: (i, 0))]
    assert sum(out_rows) == t and all(r % TM_FFN == 0 for r in out_rows)
    outs = pl.pallas_call(
        functools.partial(_ffn_kernel, n_in=len(xs), n_out=len(out_rows), n_first=n_first, final_norm=final_norm),
        grid=(t // TM_FFN,),
        in_specs=(one if len(xs) == 1 else _split_specs(n_first)) + [
            _const_spec((1, D_MODEL)),
            _const_spec((D_MODEL, 2 * D_FF)),
            _const_spec((D_FF, D_MODEL)),
            _const_spec((1, D_MODEL)),
        ],
        out_specs=one if len(out_rows) == 1 else _split_specs(n_first),
        out_shape=[jax.ShapeDtypeStruct((r, D_MODEL), F32) for r in out_rows],
        compiler_params=_params(1),
        name="ffn",
    )(*xs, g, w_in, w_out, g_final)
    return outs


def _kv_kernel(m_ref, g_ref, w_ref, k_ref, v_ref):
    mn = _rms(m_ref[0], g_ref[...]).astype(BF16)
    kv = _dot(mn, w_ref[...])
    k_ref[0] = kv[:, :D_MODEL].astype(BF16)
    v_ref[0] = kv[:, D_MODEL:].astype(BF16)


def _cross_kv(mem, g, w_kv):
    nb = mem.shape[0]
    return pl.pallas_call(
        _kv_kernel,
        grid=(nb,),
        in_specs=[
            pl.BlockSpec((1, MEM_LEN, D_MODEL), lambda i: (i, 0, 0)),
            _const_spec((1, D_MODEL)),
            _const_spec((D_MODEL, 2 * D_MODEL)),
        ],
        out_specs=[pl.BlockSpec((1, MEM_LEN, D_MODEL), lambda i: (i, 0, 0))] * 2,
        out_shape=[jax.ShapeDtypeStruct((nb, MEM_LEN, D_MODEL), BF16)] * 2,
        compiler_params=_params(1),
        name="cross_kv",
    )(mem, g, w_kv)


def _cross_body(x, g_ref, wq_ref, k_ref, v_ref, wo_ref):
    xn = _rms(x, g_ref[...]).astype(BF16)
    q = (_dot(xn, wq_ref[...]) * (X_HD ** -0.5)).astype(BF16)
    outs = []
    for h in range(X_HEADS):
        sl = slice(h * X_HD, (h + 1) * X_HD)
        s = _dot_nt(q[:, sl], k_ref[0, :, sl])
        p = jnp.exp(s - jnp.max(s, axis=-1, keepdims=True))
        l = jnp.sum(p, axis=-1, keepdims=True)
        outs.append(_dot(p.astype(BF16), v_ref[0, :, sl]) / l)
    o = jnp.concatenate(outs, axis=-1).astype(BF16)
    return x + _dot(o, wo_ref[...])


def _cross_specs(seq_bounds):
    def mem_idx(i):
        tok = i * TM_OUT
        idx = 0
        for b in seq_bounds[1:-1]:
            idx = idx + (tok >= b).astype(jnp.int32)
        return (idx, 0, 0)

    return [_const_spec((1, D_MODEL)),
            _const_spec((D_MODEL, D_MODEL)),
            pl.BlockSpec((1, MEM_LEN, D_MODEL), mem_idx),
            pl.BlockSpec((1, MEM_LEN, D_MODEL), mem_idx),
            _const_spec((D_MODEL, D_MODEL))]


def _log_sigmoid(x):
    return jnp.minimum(x, 0.0) - jnp.log1p(jnp.exp(-jnp.abs(x)))


def _gla_proj_kernel(x_ref, g_ref, w_ref, wz_ref, wg_ref, bg_ref,
                     q_ref, k_ref, v_ref, gate_ref, gf_ref, gb_ref):
    dq = GLA_HEADS * GLA_DK
    dv = GLA_HEADS * GLA_DV
    xn = _rms(x_ref[...], g_ref[...]).astype(BF16)
    q_ref[...] = _dot(xn, w_ref[:, :dq]) * (GLA_DK ** -0.5)
    k_ref[...] = _dot(xn, w_ref[:, dq:2 * dq])
    v_ref[...] = _dot(xn, w_ref[:, 2 * dq:2 * dq + dv]).astype(BF16)
    r = _dot(xn, w_ref[:, 2 * dq + dv:])
    gate_ref[...] = r * jax.nn.sigmoid(r)
    z = _dot(xn, wz_ref[...]).astype(BF16)
    pre = _dot(z, wg_ref[...]) + bg_ref[...]
    lg = _log_sigmoid(pre) * (1.0 / GLA_TAU)
    gf_ref[...] = lg[:, :dq]
    gb_ref[...] = lg[:, dq:]


def _gla_proj(x, g, w_main, w_z, w_g, b_g):
    t = x.shape[0]
    dq = GLA_HEADS * GLA_DK
    dv = GLA_HEADS * GLA_DV
    row = lambda n: pl.BlockSpec((TM_PROJ, n), lambda i: (i, 0))
    return pl.pallas_call(
        _gla_proj_kernel,
        grid=(t // TM_PROJ,),
        in_specs=[
            row(D_MODEL),
            _const_spec((1, D_MODEL)),
            _const_spec((D_MODEL, 2 * dq + 2 * dv)),
            _const_spec((D_MODEL, 2 * GLA_RANK)),
            _const_spec((2 * GLA_RANK, 2 * dq)),
            _const_spec((1, 2 * dq)),
        ],
        out_specs=[row(dq), row(dq), row(dv), row(dv), row(dq), row(dq)],
        out_shape=[
            jax.ShapeDtypeStruct((t, dq), F32),
            jax.ShapeDtypeStruct((t, dq), F32),
            jax.ShapeDtypeStruct((t, dv), BF16),
            jax.ShapeDtypeStruct((t, dv), F32),
            jax.ShapeDtypeStruct((t, dq), F32),
            jax.ShapeDtypeStruct((t, dq), F32),
        ],
        compiler_params=_params(1),
        name="gla_proj",
    )(x, g, w_main, w_z, w_g, b_g)


def _gla_tables(c):
    n_lev = c.bit_length() - 1
    half = c // 2
    idx = np.arange(c)
    tri = (idx[None, :] <= idx[:, None]).astype(np.float32)
    lvt = np.full((half, half), -2, np.int32)
    lvt[idx[:half], idx[:half]] = -1
    for i in range(half):
        for j in range(i):
            lvt[i, j] = n_lev - (i ^ j).bit_length()
    sgn = np.stack([np.where((idx // (c >> (l + 1))) % 2 == 1, 1.0, -1.0) for l in range(n_lev)])
    tri_all = np.stack([tri, tri[::-1, ::-1]])
    lvt_all = np.stack([lvt, lvt[::-1, ::-1]])
    sgn_all = np.broadcast_to(np.stack([sgn, sgn[:, ::-1]])[..., None], (2, n_lev, c, LANES))
    rm4 = np.broadcast_to((idx % 4)[:, None], (c, LANES))
    return (jnp.asarray(tri_all, BF16), jnp.asarray(sgn_all, F32), jnp.asarray(lvt_all, jnp.int32),
            jnp.asarray(rm4, jnp.int32))


def _stack_pair(x, low):
    zero = jnp.zeros_like(x)
    return jnp.concatenate([jnp.where(low, x, zero), jnp.where(low, zero, x)], axis=0)


def _gla_ref_rows(b, m, off, rm4):
    c, n = b.shape
    span = 2 * m
    if span >= 8:
        return jnp.concatenate(
            [jnp.broadcast_to(b[s + off:s + off + 1, :], (span, n)) for s in range(0, c, span)], axis=0)
    assert span == 4
    ref = b
    for rho in range(span):
        if rho != off:
            hit = rm4 == rho
            rolled = pltpu.roll(b, (rho - off) % c, 0)
            ref = jnp.concatenate([jnp.where(hit, rolled[:, j:j + LANES], ref[:, j:j + LANES])
                                   for j in range(0, n, LANES)], axis=1)
    return ref


def _gla_scan_kernel(tri_ref, sgn_ref, lvt_ref, rm4_ref,
                     qf_ref, kf_ref, vf_ref, gf_ref,
                     qb_ref, kb_ref, vb_ref, gb_ref,
                     of_ref, ob_ref, state_ref, *, resets_f, resets_b):
    c = GLA_C
    half = c // 2
    n_lev = c.bit_length() - 1
    dk, dv, dk2 = GLA_DK, GLA_DV, 2 * GLA_DK
    step = pl.program_id(0)

    def is_any(vals):
        cond = step == vals[0]
        for v in vals[1:]:
            cond = jnp.logical_or(cond, step == v)
        return cond

    @pl.when(is_any(resets_f))
    def _():
        state_ref[0] = jnp.zeros(state_ref.shape[1:], F32)

    @pl.when(is_any(resets_b))
    def _():
        state_ref[1] = jnp.zeros(state_ref.shape[1:], F32)

    low = lax.broadcasted_iota(jnp.int32, (c, dk2), 1) < dk
    low_half = lax.broadcasted_iota(jnp.int32, (half, dk2), 1) < dk
    rm4 = rm4_ref[...]
    heads = [slice(h * dk, (h + 1) * dk) for h in range(GLA_HEADS)]

    dirs = ((0, qf_ref, kf_ref, vf_ref, gf_ref, of_ref, c - 1, slice(half, c), slice(0, half)),
            (1, qb_ref, kb_ref, vb_ref, gb_ref, ob_ref, 0, slice(0, half), slice(half, c)))
    for d, q_ref, k_ref, v_ref, g_ref, o_ref, last_row, late_rows, early_rows in dirs:
        q = q_ref[...]
        k = k_ref[...]
        g = g_ref[...]
        n = q.shape[1]
        g1 = g.astype(BF16)
        r1 = g - g1.astype(F32)
        g2 = r1.astype(BF16)
        g3 = (r1 - g2.astype(F32)).astype(BF16)
        bb = _dot(tri_ref[d], jnp.concatenate([g1, g2, g3], axis=1))
        b2 = (bb[:, :n] + bb[:, n:2 * n] + bb[:, 2 * n:]) * LOG2E
        eb = jnp.exp2(b2)
        e_tot = eb[last_row:last_row + 1, :]
        qd = (q * eb).astype(BF16)
        kd = (k * jnp.exp2(b2[last_row:last_row + 1, :] - b2)).astype(BF16)
        eg = jnp.exp(g)
        lvt = lvt_ref[d]
        qk = q * k
        on_diag = lvt == -1
        dg = [[jnp.where(on_diag, jnp.sum(qk[t * half:(t + 1) * half, hs], axis=-1, keepdims=True), 0.0)
               for t in range(2)] for hs in heads]
        off = [None] * GLA_HEADS
        for l in range(n_lev):
            m = c >> (l + 1)
            s = sgn_ref[d, l]
            later = s > 0.0
            if m >= 2:
                dlt = b2 - _gla_ref_rows(b2, m, m - 1 if d == 0 else m, rm4)
                xs = [jnp.exp2(dlt[:, hs] * s) for hs in heads]
            else:
                xs = [jnp.where(later, eg[:, hs], 1.0) for hs in heads]
            w = jnp.concatenate([jnp.where(later, q[:, hs], k[:, hs]) * x for hs, x in zip(heads, xs)],
                                axis=1).astype(BF16)
            hit = lvt == l
            for p in range(GLA_HEADS // 2):
                wp = w[:, p * dk2:(p + 1) * dk2]
                if l == 0:
                    pm = _dot_nt(_stack_pair(wp[late_rows], low_half), wp[early_rows])
                    off[2 * p], off[2 * p + 1] = pm[:half], pm[half:]
                    continue
                pm = _dot_nt(_stack_pair(wp, low), wp)
                for hh in range(2):
                    for t in range(2):
                        tile = pm[hh * c + t * half:hh * c + (t + 1) * half, t * half:(t + 1) * half]
                        dg[2 * p + hh][t] = jnp.where(hit, tile, dg[2 * p + hh][t])
        zero_tile = jnp.zeros((half, half), BF16)
        for p in range(GLA_HEADS // 2):
            st = state_ref[d, p]
            inter = _dot_nt(_stack_pair(qd[:, p * dk2:(p + 1) * dk2], low), st.astype(BF16))
            vs = [v_ref[:, (2 * p + hh) * dv:(2 * p + hh + 1) * dv] for hh in range(2)]
            for hh in range(2):
                h = 2 * p + hh
                tiles = [[dg[h][0].astype(BF16), zero_tile], [zero_tile, dg[h][1].astype(BF16)]]
                if d == 0:
                    tiles[1][0] = off[h].astype(BF16)
                else:
                    tiles[0][1] = off[h].astype(BF16)
                a = jnp.concatenate([jnp.concatenate(r, axis=1) for r in tiles], axis=0)
                o_ref[:, h * dv:(h + 1) * dv] = inter[hh * c:(hh + 1) * c] + _dot(a, vs[hh])
            upd = _dot_tn(jnp.concatenate(vs, axis=0), _stack_pair(kd[:, p * dk2:(p + 1) * dk2], low))
            state_ref[d, p] = st * e_tot[:, p * dk2:(p + 1) * dk2] + upd


def _gla_scan(q, k, v, gf, gb, seq_bounds):
    t = q.shape[0]
    c = GLA_C
    n_chunks = t // c
    dq = GLA_HEADS * GLA_DK
    dv = GLA_HEADS * GLA_DV
    assert all(b % c == 0 for b in seq_bounds)
    assert c == 2 * LANES
    tables = _gla_tables(c)
    resets_f = tuple(b // c for b in seq_bounds[:-1])
    resets_b = tuple(n_chunks - b // c for b in seq_bounds[1:])
    fwd = lambda n: pl.BlockSpec((c, n), lambda i: (i, 0))
    bwd = lambda n: pl.BlockSpec((c, n), lambda i: (n_chunks - 1 - i, 0))
    return pl.pallas_call(
        functools.partial(_gla_scan_kernel, resets_f=resets_f, resets_b=resets_b),
        grid=(n_chunks,),
        in_specs=[_const_spec(tbl.shape) for tbl in tables] + [
            fwd(dq), fwd(dq), fwd(dv), fwd(dq),
            bwd(dq), bwd(dq), bwd(dv), bwd(dq),
        ],
        out_specs=[fwd(dv), bwd(dv)],
        out_shape=[jax.ShapeDtypeStruct((t, dv), F32)] * 2,
        scratch_shapes=[pltpu.VMEM((2, GLA_HEADS // 2, GLA_DV, 2 * GLA_DK), F32)],
        compiler_params=_params(1),
        name="gla_scan",
    )(*tables, q, k, v, gf, q, k, v, gb)


def _gla_out_kernel(x_ref, of_ref, ob_ref, gate_ref, gn_ref, w_ref,
                    gc_ref, wq_ref, k_ref, v_ref, wo_ref, o_ref):
    o = of_ref[...] + ob_ref[...]
    outs = []
    for h in range(GLA_HEADS):
        sl = slice(h * GLA_DV, (h + 1) * GLA_DV)
        outs.append(_rms(o[:, sl], gn_ref[:, sl]))
    y = (jnp.concatenate(outs, axis=-1) * gate_ref[...]).astype(BF16)
    x = x_ref[...] + _dot(y, w_ref[...])
    o_ref[...] = _cross_body(x, gc_ref, wq_ref, k_ref, v_ref, wo_ref)


def _gla_out(x, o_f, o_b, gate, g_norm, w_out, cross_args, seq_bounds):
    t = x.shape[0]
    row = pl.BlockSpec((TM_OUT, D_MODEL), lambda i: (i, 0))
    return pl.pallas_call(
        _gla_out_kernel,
        grid=(t // TM_OUT,),
        in_specs=[row, row, row, row, _const_spec((1, D_MODEL)), _const_spec((D_MODEL, D_MODEL))]
        + _cross_specs(seq_bounds),
        out_specs=row,
        out_shape=jax.ShapeDtypeStruct((t, D_MODEL), F32),
        compiler_params=_params(1),
        name="gla_out_cross",
    )(x, o_f, o_b, gate, g_norm, w_out, *cross_args)


def _dil_proj_kernel(x_ref, g_ref, w_ref, o0_ref, o1_ref, o2_ref, xs_ref):
    o_refs = (o0_ref, o1_ref, o2_ref)
    n_chunk = D_MODEL // LANES
    for j in range(n_chunk):
        xs_ref[j] = x_ref[:, j * LANES:(j + 1) * LANES]
    for g, (_, r) in enumerate(DIL_GROUPS):
        n = TM_DIL // r
        if r == 1:
            xr = x_ref[...]
        else:
            xr = jnp.concatenate(
                [jnp.concatenate([xs_ref[j, pl.ds(rho, n, stride=r), :] for j in range(n_chunk)], axis=1)
                 for rho in range(r)], axis=0)
        xn = _rms(xr, g_ref[...]).astype(BF16)
        for c in range(3):
            col = (c * N_DIL + g) * D_MODEL
            y = _dot(xn, w_ref[:, col:col + D_MODEL])
            if c == 0:
                y = y * (DIL_HD ** -0.5)
            y = y.astype(BF16)
            for rho in range(r):
                o_refs[g][rho, :, c * D_MODEL:(c + 1) * D_MODEL] = y[rho * n:(rho + 1) * n]


def _dil_proj(x, g, w_qkv):
    t = x.shape[0]
    n = 3 * N_DIL * D_MODEL
    return pl.pallas_call(
        _dil_proj_kernel,
        grid=(t // TM_DIL,),
        in_specs=[pl.BlockSpec((TM_DIL, D_MODEL), lambda i: (i, 0)), _const_spec((1, D_MODEL)),
                  _const_spec((D_MODEL, n))],
        out_specs=[pl.BlockSpec((r, TM_DIL // r, 3 * D_MODEL), lambda i: (0, i, 0)) for _, r in DIL_GROUPS],
        out_shape=[jax.ShapeDtypeStruct((r, t // r, 3 * D_MODEL), BF16) for _, r in DIL_GROUPS],
        scratch_shapes=[pltpu.VMEM((D_MODEL // LANES, TM_DIL, LANES), F32)],
        compiler_params=_params(1),
        name="dil_proj",
    )(x, g, w_qkv)


def _t5_bucket(rel):
    half = NUM_BUCKETS // 2
    max_exact = half // 2
    ret = (rel > 0).astype(np.int32) * half
    n = np.abs(rel)
    large = max_exact + (np.log(np.maximum(n, 1) / max_exact) / np.log(MAX_DISTANCE / max_exact)
                         * (half - max_exact)).astype(np.int32)
    large = np.minimum(large, half - 1)
    return (ret + np.where(n < max_exact, n, large)).astype(np.int32)


def _dil_bias_table(rel_bias_cols, r):
    w, qb, hg = DIL_W, DIL_QB, DIL_HG
    rel = np.arange(qb + 2 * w)[None, :] - w - np.arange(qb)[:, None]
    band = np.abs(rel) <= w
    bucket = _t5_bucket(np.clip(rel, -w, w) * r).reshape(-1)
    onehot = np.zeros((NUM_BUCKETS, bucket.size), np.float32)
    onehot[bucket, np.arange(bucket.size)] = 1.0
    bias = lax.dot_general(rel_bias_cols.T, jnp.asarray(onehot), (((1,), (0,)), ((), ())),
                           precision=lax.Precision.HIGHEST).reshape(DIL_HEADS, qb, qb + 2 * w)
    key = np.arange(qb + 2 * w)[None, :]
    tables = []
    for var in range(4):
        ok = band & ~((key < w) & bool(var & 1)) & ~((key >= qb + w) & bool(var & 2))
        tables.append(jnp.where(ok[None], bias, NEG).reshape(DIL_HEADS // hg, hg * qb, qb + 2 * w))
    return jnp.stack(tables)


def _dil_attn_kernel(bias_ref, q_ref, kp_ref, km_ref, kn_ref, vp_ref, vm_ref, vn_ref,
                     o_ref, lse_ref, kx_ref, vx_ref, *, sub_bounds):
    tq, w, qb, hg = DIL_TQ, DIL_W, DIL_QB, DIL_HG
    gl = hg * DIL_HD
    i = pl.program_id(1)
    kx_ref[0:w] = kp_ref[...]
    kx_ref[w:w + tq] = km_ref[...]
    kx_ref[w + tq:] = kn_ref[...]
    vx_ref[0:w] = vp_ref[...]
    vx_ref[w:w + tq] = vm_ref[...]
    vx_ref[w + tq:] = vn_ref[...]
    start = i * tq
    lo = jnp.int32(0)
    hi = jnp.int32(sub_bounds[1])
    for b0, b1 in zip(sub_bounds[1:-1], sub_bounds[2:]):
        inside = start >= b0
        lo = jnp.where(inside, b0, lo)
        hi = jnp.where(inside, b1, hi)
    head_of_lane = lax.broadcasted_iota(jnp.int32, (qb, gl), 1) // DIL_HD
    lane = lax.broadcasted_iota(jnp.int32, (qb, LANES), 1)

    for jb in range(tq // qb):
        rows = slice(jb * qb, (jb + 1) * qb)
        krows = slice(jb * qb, (jb + 1) * qb + 2 * w)
        q0 = start + jb * qb
        var = (q0 == lo).astype(jnp.int32) + 2 * (q0 + qb == hi).astype(jnp.int32)
        lse_acc = jnp.zeros((qb, LANES), F32)
        for p in range(DIL_HEADS // hg):
            sl = slice(p * gl, (p + 1) * gl)
            qg = q_ref[rows, sl]
            zero = jnp.zeros_like(qg)
            qs = jnp.concatenate([jnp.where(head_of_lane == h, qg, zero) for h in range(hg)], axis=0)
            s = _dot_nt(qs, kx_ref[krows, sl]) + bias_ref[var, p]
            mx = jnp.max(s, axis=-1, keepdims=True)
            pr = jnp.exp(s - mx)
            l = jnp.sum(pr, axis=-1, keepdims=True)
            pv = _dot(pr.astype(BF16), vx_ref[krows, sl]) / l
            lse = mx + jnp.log(l)
            og = pv[:qb]
            for h in range(hg):
                if h:
                    og = jnp.where(head_of_lane == h, pv[h * qb:(h + 1) * qb], og)
                lse_acc = jnp.where(lane == p * hg + h, lse[h * qb:(h + 1) * qb], lse_acc)
            o_ref[rows, sl] = og
        lse_ref[rows, :] = lse_acc


def _dil_attn(qkv, bias, r, seq_bounds):
    rows = qkv.shape[1]
    tq, w = DIL_TQ, DIL_W
    n_blk = rows // tq
    n_w = rows // w
    per = tq // w
    sub_bounds = tuple(b // r for b in seq_bounds)
    assert all(b % tq == 0 for b in sub_bounds)

    def main(col):
        return pl.BlockSpec((None, tq, D_MODEL), lambda rho, i: (rho, i, col))

    def prev(col):
        return pl.BlockSpec((None, w, D_MODEL), lambda rho, i: (rho, jnp.maximum(i * per - 1, 0), col))

    def nxt(col):
        return pl.BlockSpec((None, w, D_MODEL), lambda rho, i: (rho, jnp.minimum((i + 1) * per, n_w - 1), col))

    return pl.pallas_call(
        functools.partial(_dil_attn_kernel, sub_bounds=sub_bounds),
        grid=(r, n_blk),
        in_specs=[_const_spec(bias.shape), main(0), prev(1), main(1), nxt(1), prev(2), main(2), nxt(2)],
        out_specs=[pl.BlockSpec((None, tq, D_MODEL), lambda rho, i: (rho, i, 0)),
                   pl.BlockSpec((None, tq, LANES), lambda rho, i: (rho, i, 0))],
        out_shape=[jax.ShapeDtypeStruct((r, rows, D_MODEL), F32),
                   jax.ShapeDtypeStruct((r, rows, LANES), F32)],
        scratch_shapes=[pltpu.VMEM((tq + 2 * w, D_MODEL), BF16)] * 2,
        compiler_params=_params(2),
        name=f"dil_attn_r{r}",
    )(bias, qkv, qkv, qkv, qkv, qkv, qkv, qkv)


def _dil_out_kernel(x_ref, o0_ref, o1_ref, o2_ref, l0_ref, l1_ref, l2_ref, ex_ref, w_ref,
                    gc_ref, wq_ref, k_ref, v_ref, wo_ref, out_ref, os_ref, ls_ref):
    def token_order(src_ref, dst_ref, r):
        if r == 1:
            return src_ref[0]
        n_chunk = src_ref.shape[-1] // LANES
        for rho in range(r):
            for j in range(n_chunk):
                dst_ref[j, pl.ds(rho, TM_OUT // r, stride=r), :] = src_ref[rho, :, j * LANES:(j + 1) * LANES]
        return jnp.concatenate([dst_ref[j] for j in range(n_chunk)], axis=1)

    rs = [r for _, r in DIL_GROUPS]
    lses = [token_order(l_ref, ls_ref.at[g], r) for g, (l_ref, r) in enumerate(zip((l0_ref, l1_ref, l2_ref), rs))]
    mx = jnp.maximum(jnp.maximum(lses[0], lses[1]), lses[2])
    es = [jnp.exp(l - mx) for l in lses]
    inv = 1.0 / (es[0] + es[1] + es[2])
    ex = ex_ref[...]
    acc = None
    for g, (e, o_ref, r) in enumerate(zip(es, (o0_ref, o1_ref, o2_ref), rs)):
        wt = e * inv
        hi = wt.astype(BF16)
        lo = (wt - hi.astype(F32)).astype(BF16)
        term = (_dot(hi, ex) + _dot(lo, ex)) * token_order(o_ref, os_ref, r)
        acc = term if acc is None else acc + term
    x = x_ref[...] + _dot(acc.astype(BF16), w_ref[...])
    out_ref[...] = _cross_body(x, gc_ref, wq_ref, k_ref, v_ref, wo_ref)


def _dil_out(x, outs, lses, w_out, cross_args, seq_bounds):
    t = x.shape[0]
    expand = np.zeros((LANES, D_MODEL), np.float32)
    for h in range(DIL_HEADS):
        expand[h, h * DIL_HD:(h + 1) * DIL_HD] = 1.0
    row = pl.BlockSpec((TM_OUT, D_MODEL), lambda i: (i, 0))
    res = lambda n: [pl.BlockSpec((r, TM_OUT // r, n), lambda i: (0, i, 0)) for _, r in DIL_GROUPS]
    return pl.pallas_call(
        _dil_out_kernel,
        grid=(t // TM_OUT,),
        in_specs=[row] + res(D_MODEL) + res(LANES) + [_const_spec((LANES, D_MODEL)), _const_spec((D_MODEL, D_MODEL))]
        + _cross_specs(seq_bounds),
        out_specs=row,
        out_shape=jax.ShapeDtypeStruct((t, D_MODEL), F32),
        scratch_shapes=[pltpu.VMEM((D_MODEL // LANES, TM_OUT, LANES), F32),
                        pltpu.VMEM((N_DIL, 1, TM_OUT, LANES), F32)],
        compiler_params=_params(1),
        name="dil_out_cross",
    )(x, *outs, *lses, jnp.asarray(expand, BF16), w_out, *cross_args)


def kernel(x_prompt, x_sample, mem_prompt, mem_sample, rel_bias, norm_ffn1, ffn1_in, ffn1_out, norm_mix,
           gla_w_in, gla_wg_f, gla_bg_f, gla_wg_b, gla_bg_b, gla_norm, gla_w_out, dil_w_qkv, dil_w_out,
           norm_cross, norm_mem, cross_w_q, cross_w_kv, cross_w_o, norm_ffn2, ffn2_in, ffn2_out, norm_final):
    bp, sp, _ = x_prompt.shape
    bs, ss, _ = x_sample.shape
    depth = norm_ffn1.shape[0]
    xs = [x_prompt.reshape(bp * sp, D_MODEL), x_sample.reshape(bs * ss, D_MODEL)]
    t = bp * sp + bs * ss
    mem = jnp.concatenate([mem_prompt, mem_sample], axis=0)
    seq_bounds = tuple(i * sp for i in range(bp)) + tuple(bp * sp + i * ss for i in range(bs + 1))
    row = lambda v: v.reshape(1, -1)
    bf = lambda v: v.astype(BF16)
    dq = GLA_HEADS * GLA_DK
    dv = GLA_HEADS * GLA_DV

    for i in range(depth):
        x, = _ffn(xs, row(norm_ffn1[i]), bf(ffn1_in[i]), bf(ffn1_out[i]), row(norm_final), False, [t])
        k_mem, v_mem = _cross_kv(mem, row(norm_mem[i]), bf(cross_w_kv[i]))
        cross_args = (row(norm_cross[i]), bf(cross_w_q[i]), k_mem, v_mem, bf(cross_w_o[i]))
        j = i // 2
        if i % 2 == 0:
            w_in = gla_w_in[j]
            zero = jnp.zeros((GLA_RANK, dq), F32)
            w_g = jnp.concatenate([jnp.concatenate([gla_wg_f[j], zero], axis=1),
                                   jnp.concatenate([zero, gla_wg_b[j]], axis=1)], axis=0)
            b_g = jnp.concatenate([gla_bg_f[j], gla_bg_b[j]]).reshape(1, 2 * dq)
            q, k, v, gate, gf, gb = _gla_proj(x, row(norm_mix[i]), bf(w_in[:, :2 * dq + 2 * dv]),
                                              bf(w_in[:, 2 * dq + 2 * dv:]), bf(w_g), b_g)
            o_f, o_b = _gla_scan(q, k, v, gf, gb, seq_bounds)
            x = _gla_out(x, o_f, o_b, gate, row(gla_norm[j]), bf(gla_w_out[j]), cross_args, seq_bounds)
        else:
            qkvs = _dil_proj(x, row(norm_mix[i]), bf(dil_w_qkv[j]))
            outs, lses = [], []
            for g, (_, r) in enumerate(DIL_GROUPS):
                bias = _dil_bias_table(rel_bias[:, g * DIL_HEADS:(g + 1) * DIL_HEADS], r)
                o, lse = _dil_attn(qkvs[g], bias, r, seq_bounds)
                outs.append(o)
                lses.append(lse)
            x = _dil_out(x, outs, lses, bf(dil_w_out[j]), cross_args, seq_bounds)
        last = i == depth - 1
        xs = _ffn([x], row(norm_ffn2[i]), bf(ffn2_in[i]), bf(ffn2_out[i]), row(norm_final), last,
                  [bp * sp, bs * ss] if last else [t])

    y_prompt, y_sample = xs
    return (y_prompt.reshape(bp, sp, D_MODEL), y_sample.reshape(bs, ss, D_MODEL))
```

```python
import functools

import numpy as np
import jax
import jax.numpy as jnp
from jax import lax
from jax.experimental import pallas as pl
from jax.experimental.pallas import tpu as pltpu

F32 = jnp.float32
BF16 = jnp.bfloat16

D_MODEL = 1024
D_FF = 2816
EPS = 1e-6
MEM_LEN = 256
GLA_HEADS = 4
GLA_DK = 128
GLA_DV = 256
GLA_RANK = 16
GLA_TAU = 16.0
DIL_GROUPS = ((128, 1), (512, 4), (2048, 16))
N_DIL = len(DIL_GROUPS)
DIL_HEADS = 16
DIL_HD = 64
DIL_W = 64
NUM_BUCKETS = 32
MAX_DISTANCE = 1024
X_HEADS = 4
X_HD = 256
NEG = -1e30
LOG2E = 1.4426950408889634

LANES = 128
VMEM_LIMIT = 56 * 1024 * 1024

TM_FFN = 512
TM_PROJ = 1024
TM_DIL = 256
TM_OUT = 512
GLA_C = 256
DIL_TQ = 512
DIL_QB = 128
DIL_HG = 4


def _params(n_axes, vmem=VMEM_LIMIT):
    return pltpu.CompilerParams(dimension_semantics=("arbitrary",) * n_axes, vmem_limit_bytes=vmem)


def _const_spec(shape):
    nd = len(shape)
    return pl.BlockSpec(shape, lambda *_: (0,) * nd, pipeline_mode=pl.Buffered(1))


def _rms(x, g):
    return x * lax.rsqrt(jnp.mean(x * x, axis=-1, keepdims=True) + EPS) * g


def _dot(a, b):
    return jnp.dot(a, b, preferred_element_type=F32)


def _dot_nt(a, b):
    return lax.dot_general(a, b, (((1,), (1,)), ((), ())), preferred_element_type=F32)


def _dot_tn(a, b):
    return lax.dot_general(a, b, (((0,), (0,)), ((), ())), preferred_element_type=F32)


def _ffn_kernel(*refs, n_in, n_out, n_first, final_norm):
    x_refs = refs[:n_in]
    g_ref, win_ref, wout_ref, gfin_ref = refs[n_in:n_in + 4]
    o_refs = refs[n_in + 4:]
    i = pl.program_id(0)
    x = x_refs[0][...] if n_in == 1 else jnp.where(i < n_first, x_refs[0][...], x_refs[1][...])
    xn = _rms(x, g_ref[...]).astype(BF16)
    a = _dot(xn, win_ref[:, :D_FF])
    b = _dot(xn, win_ref[:, D_FF:])
    h = (a * jax.nn.sigmoid(a) * b).astype(BF16)
    y = x + 0.5 * _dot(h, wout_ref[...])
    if final_norm:
        y = _rms(y, gfin_ref[...])
    if n_out == 1:
        o_refs[0][...] = y
    else:
        @pl.when(i < n_first)
        def _():
            o_refs[0][...] = y

        @pl.when(i >= n_first)
        def _():
            o_refs[1][...] = y


def _split_specs(n_first):
    return [pl.BlockSpec((TM_FFN, D_MODEL), lambda i: (jnp.minimum(i, n_first - 1), 0)),
            pl.BlockSpec((TM_FFN, D_MODEL), lambda i: (jnp.maximum(i - n_first, 0), 0))]


def _ffn(xs, g, w_in, w_out, g_final, final_norm, out_rows):
    t = sum(x.shape[0] for x in xs)
    n_first = (xs[0].shape[0] if len(xs) == 2 else out_rows[0]) // TM_FFN
    one = [pl.BlockSpec((TM_FFN, D_MODEL), lambda i: (i, 0))]
    assert sum(out_rows) == t and all(r % TM_FFN == 0 for r in out_rows)
    outs = pl.pallas_call(
        functools.partial(_ffn_kernel, n_in=len(xs), n_out=len(out_rows), n_first=n_first, final_norm=final_norm),
        grid=(t // TM_FFN,),
        in_specs=(one if len(xs) == 1 else _split_specs(n_first)) + [
            _const_spec((1, D_MODEL)),
            _const_spec((D_MODEL, 2 * D_FF)),
            _const_spec((D_FF, D_MODEL)),
            _const_spec((1, D_MODEL)),
        ],
        out_specs=one if len(out_rows) == 1 else _split_specs(n_first),
        out_shape=[jax.ShapeDtypeStruct((r, D_MODEL), F32) for r in out_rows],
        compiler_params=_params(1),
        name="ffn",
    )(*xs, g, w_in, w_out, g_final)
    return outs


def _kv_kernel(m_ref, g_ref, w_ref, k_ref, v_ref):
    mn = _rms(m_ref[0], g_ref[...]).astype(BF16)
    kv = _dot(mn, w_ref[...])
    k_ref[0] = kv[:, :D_MODEL].astype(BF16)
    v_ref[0] = kv[:, D_MODEL:].astype(BF16)


def _cross_kv(mem, g, w_kv):
    nb = mem.shape[0]
    return pl.pallas_call(
        _kv_kernel,
        grid=(nb,),
        in_specs=[
            pl.BlockSpec((1, MEM_LEN, D_MODEL), lambda i: (i, 0, 0)),
            _const_spec((1, D_MODEL)),
            _const_spec((D_MODEL, 2 * D_MODEL)),
        ],
        out_specs=[pl.BlockSpec((1, MEM_LEN, D_MODEL), lambda i: (i, 0, 0))] * 2,
        out_shape=[jax.ShapeDtypeStruct((nb, MEM_LEN, D_MODEL), BF16)] * 2,
        compiler_params=_params(1),
        name="cross_kv",
    )(mem, g, w_kv)


def _cross_body(x, g_ref, wq_ref, k_ref, v_ref, wo_ref):
    xn = _rms(x, g_ref[...]).astype(BF16)
    q = (_dot(xn, wq_ref[...]) * (X_HD ** -0.5)).astype(BF16)
    outs = []
    for h in range(X_HEADS):
        sl = slice(h * X_HD, (h + 1) * X_HD)
        s = _dot_nt(q[:, sl], k_ref[0, :, sl])
        p = jnp.exp(s - jnp.max(s, axis=-1, keepdims=True))
        l = jnp.sum(p, axis=-1, keepdims=True)
        outs.append(_dot(p.astype(BF16), v_ref[0, :, sl]) / l)
    o = jnp.concatenate(outs, axis=-1).astype(BF16)
    return x + _dot(o, wo_ref[...])


def _cross_specs(seq_bounds):
    def mem_idx(i):
        tok = i * TM_OUT
        idx = 0
        for b in seq_bounds[1:-1]:
            idx = idx + (tok >= b).astype(jnp.int32)
        return (idx, 0, 0)

    return [_const_spec((1, D_MODEL)),
            _const_spec((D_MODEL, D_MODEL)),
            pl.BlockSpec((1, MEM_LEN, D_MODEL), mem_idx),
            pl.BlockSpec((1, MEM_LEN, D_MODEL), mem_idx),
            _const_spec((D_MODEL, D_MODEL))]


def _log_sigmoid(x):
    return jnp.minimum(x, 0.0) - jnp.log1p(jnp.exp(-jnp.abs(x)))


def _gla_proj_kernel(x_ref, g_ref, w_ref, wz_ref, wg_ref, bg_ref,
                     q_ref, k_ref, v_ref, gate_ref, gf_ref, gb_ref):
    dq = GLA_HEADS * GLA_DK
    dv = GLA_HEADS * GLA_DV
    xn = _rms(x_ref[...], g_ref[...]).astype(BF16)
    z = _dot(xn, wz_ref[...]).astype(BF16)
    pre = _dot(z, wg_ref[...]) + bg_ref[...]
    lg = _log_sigmoid(pre) * (1.0 / GLA_TAU)
    gf_ref[...] = lg[:, :dq]
    gb_ref[...] = lg[:, dq:]
    r = _dot(xn, w_ref[:, 2 * dq + dv:])
    gate_ref[...] = r * jax.nn.sigmoid(r)
    q_ref[...] = _dot(xn, w_ref[:, :dq]) * (GLA_DK ** -0.5)
    k_ref[...] = _dot(xn, w_ref[:, dq:2 * dq])
    v_ref[...] = _dot(xn, w_ref[:, 2 * dq:2 * dq + dv]).astype(BF16)


def _gla_proj(x, g, w_main, w_z, w_g, b_g):
    t = x.shape[0]
    dq = GLA_HEADS * GLA_DK
    dv = GLA_HEADS * GLA_DV
    row = lambda n: pl.BlockSpec((TM_PROJ, n), lambda i: (i, 0))
    return pl.pallas_call(
        _gla_proj_kernel,
        grid=(t // TM_PROJ,),
        in_specs=[
            row(D_MODEL),
            _const_spec((1, D_MODEL)),
            _const_spec((D_MODEL, 2 * dq + 2 * dv)),
            _const_spec((D_MODEL, 2 * GLA_RANK)),
            _const_spec((2 * GLA_RANK, 2 * dq)),
            _const_spec((1, 2 * dq)),
        ],
        out_specs=[row(dq), row(dq), row(dv), row(dv), row(dq), row(dq)],
        out_shape=[
            jax.ShapeDtypeStruct((t, dq), F32),
            jax.ShapeDtypeStruct((t, dq), F32),
            jax.ShapeDtypeStruct((t, dv), BF16),
            jax.ShapeDtypeStruct((t, dv), F32),
            jax.ShapeDtypeStruct((t, dq), F32),
            jax.ShapeDtypeStruct((t, dq), F32),
        ],
        compiler_params=_params(1),
        name="gla_proj",
    )(x, g, w_main, w_z, w_g, b_g)


def _gla_tables(c):
    n_lev = c.bit_length() - 1
    half = c // 2
    idx = np.arange(c)
    tri = (idx[None, :] <= idx[:, None]).astype(np.float32)
    lvt = np.full((half, half), -2, np.int32)
    lvt[idx[:half], idx[:half]] = -1
    for i in range(half):
        for j in range(i):
            lvt[i, j] = n_lev - (i ^ j).bit_length()
    sgn = np.stack([np.where((idx // (c >> (l + 1))) % 2 == 1, 1.0, -1.0) for l in range(n_lev)])
    tri_all = np.stack([tri, tri[::-1, ::-1]])
    lvt_all = np.stack([lvt, lvt[::-1, ::-1]])
    sgn_all = np.broadcast_to(np.stack([sgn, sgn[:, ::-1]])[..., None], (2, n_lev, c, LANES))
    rm4 = np.broadcast_to((idx % 4)[:, None], (c, LANES))
    return (jnp.asarray(tri_all, BF16), jnp.asarray(sgn_all, F32), jnp.asarray(lvt_all, jnp.int32),
            jnp.asarray(rm4, jnp.int32))


def _stack_pair(x, low):
    zero = jnp.zeros_like(x)
    return jnp.concatenate([jnp.where(low, x, zero), jnp.where(low, zero, x)], axis=0)


def _gla_ref_rows(b, m, off, rm4):
    c, n = b.shape
    span = 2 * m
    if span >= 8:
        return jnp.concatenate(
            [jnp.broadcast_to(b[s + off:s + off + 1, :], (span, n)) for s in range(0, c, span)], axis=0)
    assert span == 4
    ref = b
    for rho in range(span):
        if rho != off:
            hit = rm4 == rho
            rolled = pltpu.roll(b, (rho - off) % c, 0)
            ref = jnp.concatenate([jnp.where(hit, rolled[:, j:j + LANES], ref[:, j:j + LANES])
                                   for j in range(0, n, LANES)], axis=1)
    return ref


def _gla_scan_kernel(tri_ref, sgn_ref, lvt_ref, rm4_ref,
                     qf_ref, kf_ref, vf_ref, gf_ref,
                     qb_ref, kb_ref, vb_ref, gb_ref,
                     of_ref, ob_ref, state_ref, *, resets_f, resets_b):
    c = GLA_C
    half = c // 2
    n_lev = c.bit_length() - 1
    dk, dv, dk2 = GLA_DK, GLA_DV, 2 * GLA_DK
    step = pl.program_id(0)

    def is_any(vals):
        cond = step == vals[0]
        for v in vals[1:]:
            cond = jnp.logical_or(cond, step == v)
        return cond

    @pl.when(is_any(resets_f))
    def _():
        state_ref[0] = jnp.zeros(state_ref.shape[1:], F32)

    @pl.when(is_any(resets_b))
    def _():
        state_ref[1] = jnp.zeros(state_ref.shape[1:], F32)

    low = lax.broadcasted_iota(jnp.int32, (c, dk2), 1) < dk
    low_half = lax.broadcasted_iota(jnp.int32, (half, dk2), 1) < dk
    rm4 = rm4_ref[...]
    heads = [slice(h * dk, (h + 1) * dk) for h in range(GLA_HEADS)]

    dirs = ((0, qf_ref, kf_ref, vf_ref, gf_ref, of_ref, c - 1, slice(half, c), slice(0, half)),
            (1, qb_ref, kb_ref, vb_ref, gb_ref, ob_ref, 0, slice(0, half), slice(half, c)))
    for d, q_ref, k_ref, v_ref, g_ref, o_ref, last_row, late_rows, early_rows in dirs:
        q = q_ref[...]
        k = k_ref[...]
        g = g_ref[...]
        n = q.shape[1]
        g1 = g.astype(BF16)
        r1 = g - g1.astype(F32)
        g2 = r1.astype(BF16)
        g3 = (r1 - g2.astype(F32)).astype(BF16)
        bb = _dot(tri_ref[d], jnp.concatenate([g1, g2, g3], axis=1))
        b2 = (bb[:, :n] + bb[:, n:2 * n] + bb[:, 2 * n:]) * LOG2E
        eb = jnp.exp2(b2)
        e_tot = eb[last_row:last_row + 1, :]
        qd = (q * eb).astype(BF16)
        kd = (k * jnp.exp2(b2[last_row:last_row + 1, :] - b2)).astype(BF16)
        eg = jnp.exp(g)
        lvt = lvt_ref[d]
        qk = q * k
        on_diag = lvt == -1
        dg = [[jnp.where(on_diag, jnp.sum(qk[t * half:(t + 1) * half, hs], axis=-1, keepdims=True), 0.0)
               for t in range(2)] for hs in heads]
        off = [None] * GLA_HEADS
        for l in range(n_lev):
            m = c >> (l + 1)
            s = sgn_ref[d, l]
            later = s > 0.0
            if m >= 2:
                dlt = b2 - _gla_ref_rows(b2, m, m - 1 if d == 0 else m, rm4)
                xs = [jnp.exp2(dlt[:, hs] * s) for hs in heads]
            else:
                xs = [jnp.where(later, eg[:, hs], 1.0) for hs in heads]
            w = jnp.concatenate([jnp.where(later, q[:, hs], k[:, hs]) * x for hs, x in zip(heads, xs)],
                                axis=1).astype(BF16)
            hit = lvt == l
            for p in range(GLA_HEADS // 2):
                wp = w[:, p * dk2:(p + 1) * dk2]
                if l == 0:
                    pm = _dot_nt(_stack_pair(wp[late_rows], low_half), wp[early_rows])
                    off[2 * p], off[2 * p + 1] = pm[:half], pm[half:]
                    continue
                pm = _dot_nt(_stack_pair(wp, low), wp)
                for hh in range(2):
                    for t in range(2):
                        tile = pm[hh * c + t * half:hh * c + (t + 1) * half, t * half:(t + 1) * half]
                        dg[2 * p + hh][t] = jnp.where(hit, tile, dg[2 * p + hh][t])
        zero_tile = jnp.zeros((half, half), BF16)
        for p in range(GLA_HEADS // 2):
            st = state_ref[d, p]
            inter = _dot_nt(_stack_pair(qd[:, p * dk2:(p + 1) * dk2], low), st.astype(BF16))
            vs = [v_ref[:, (2 * p + hh) * dv:(2 * p + hh + 1) * dv] for hh in range(2)]
            for hh in range(2):
                h = 2 * p + hh
                tiles = [[dg[h][0].astype(BF16), zero_tile], [zero_tile, dg[h][1].astype(BF16)]]
                if d == 0:
                    tiles[1][0] = off[h].astype(BF16)
                else:
                    tiles[0][1] = off[h].astype(BF16)
                a = jnp.concatenate([jnp.concatenate(r, axis=1) for r in tiles], axis=0)
                o_ref[:, h * dv:(h + 1) * dv] = inter[hh * c:(hh + 1) * c] + _dot(a, vs[hh])
            upd = _dot_tn(jnp.concatenate(vs, axis=0), _stack_pair(kd[:, p * dk2:(p + 1) * dk2], low))
            state_ref[d, p] = st * e_tot[:, p * dk2:(p + 1) * dk2] + upd


def _gla_scan(q, k, v, gf, gb, seq_bounds):
    t = q.shape[0]
    c = GLA_C
    n_chunks = t // c
    dq = GLA_HEADS * GLA_DK
    dv = GLA_HEADS * GLA_DV
    assert all(b % c == 0 for b in seq_bounds)
    assert c == 2 * LANES
    tables = _gla_tables(c)
    resets_f = tuple(b // c for b in seq_bounds[:-1])
    resets_b = tuple(n_chunks - b // c for b in seq_bounds[1:])
    fwd = lambda n: pl.BlockSpec((c, n), lambda i: (i, 0))
    bwd = lambda n: pl.BlockSpec((c, n), lambda i: (n_chunks - 1 - i, 0))
    return pl.pallas_call(
        functools.partial(_gla_scan_kernel, resets_f=resets_f, resets_b=resets_b),
        grid=(n_chunks,),
        in_specs=[_const_spec(tbl.shape) for tbl in tables] + [
            fwd(dq), fwd(dq), fwd(dv), fwd(dq),
            bwd(dq), bwd(dq), bwd(dv), bwd(dq),
        ],
        out_specs=[fwd(dv), bwd(dv)],
        out_shape=[jax.ShapeDtypeStruct((t, dv), F32)] * 2,
        scratch_shapes=[pltpu.VMEM((2, GLA_HEADS // 2, GLA_DV, 2 * GLA_DK), F32)],
        compiler_params=_params(1),
        name="gla_scan",
    )(*tables, q, k, v, gf, q, k, v, gb)


def _gla_out_kernel(x_ref, of_ref, ob_ref, gate_ref, gn_ref, w_ref,
                    gc_ref, wq_ref, k_ref, v_ref, wo_ref, o_ref):
    o = of_ref[...] + ob_ref[...]
    outs = []
    for h in range(GLA_HEADS):
        sl = slice(h * GLA_DV, (h + 1) * GLA_DV)
        outs.append(_rms(o[:, sl], gn_ref[:, sl]))
    y = (jnp.concatenate(outs, axis=-1) * gate_ref[...]).astype(BF16)
    x = x_ref[...] + _dot(y, w_ref[...])
    o_ref[...] = _cross_body(x, gc_ref, wq_ref, k_ref, v_ref, wo_ref)


def _gla_out(x, o_f, o_b, gate, g_norm, w_out, cross_args, seq_bounds):
    t = x.shape[0]
    row = pl.BlockSpec((TM_OUT, D_MODEL), lambda i: (i, 0))
    return pl.pallas_call(
        _gla_out_kernel,
        grid=(t // TM_OUT,),
        in_specs=[row, row, row, row, _const_spec((1, D_MODEL)), _const_spec((D_MODEL, D_MODEL))]
        + _cross_specs(seq_bounds),
        out_specs=row,
        out_shape=jax.ShapeDtypeStruct((t, D_MODEL), F32),
        compiler_params=_params(1),
        name="gla_out_cross",
    )(x, o_f, o_b, gate, g_norm, w_out, *cross_args)


def _dil_proj_kernel(x_ref, g_ref, w_ref, o0_ref, o1_ref, o2_ref, xs_ref):
    o_refs = (o0_ref, o1_ref, o2_ref)
    n_chunk = D_MODEL // LANES
    for j in range(n_chunk):
        xs_ref[j] = x_ref[:, j * LANES:(j + 1) * LANES]
    for g, (_, r) in enumerate(DIL_GROUPS):
        n = TM_DIL // r
        if r == 1:
            xr = x_ref[...]
        else:
            xr = jnp.concatenate(
                [jnp.concatenate([xs_ref[j, pl.ds(rho, n, stride=r), :] for j in range(n_chunk)], axis=1)
                 for rho in range(r)], axis=0)
        xn = _rms(xr, g_ref[...]).astype(BF16)
        for c in range(3):
            col = (c * N_DIL + g) * D_MODEL
            y = _dot(xn, w_ref[:, col:col + D_MODEL])
            if c == 0:
                y = y * (DIL_HD ** -0.5 * LOG2E)
            y = y.astype(BF16)
            for rho in range(r):
                o_refs[g][rho, :, c * D_MODEL:(c + 1) * D_MODEL] = y[rho * n:(rho + 1) * n]


def _dil_proj(x, g, w_qkv):
    t = x.shape[0]
    n = 3 * N_DIL * D_MODEL
    return pl.pallas_call(
        _dil_proj_kernel,
        grid=(t // TM_DIL,),
        in_specs=[pl.BlockSpec((TM_DIL, D_MODEL), lambda i: (i, 0)), _const_spec((1, D_MODEL)),
                  _const_spec((D_MODEL, n))],
        out_specs=[pl.BlockSpec((r, TM_DIL // r, 3 * D_MODEL), lambda i: (0, i, 0)) for _, r in DIL_GROUPS],
        out_shape=[jax.ShapeDtypeStruct((r, t // r, 3 * D_MODEL), BF16) for _, r in DIL_GROUPS],
        scratch_shapes=[pltpu.VMEM((D_MODEL // LANES, TM_DIL, LANES), F32)],
        compiler_params=_params(1),
        name="dil_proj",
    )(x, g, w_qkv)


def _t5_bucket(rel):
    half = NUM_BUCKETS // 2
    max_exact = half // 2
    ret = (rel > 0).astype(np.int32) * half
    n = np.abs(rel)
    large = max_exact + (np.log(np.maximum(n, 1) / max_exact) / np.log(MAX_DISTANCE / max_exact)
                         * (half - max_exact)).astype(np.int32)
    large = np.minimum(large, half - 1)
    return (ret + np.where(n < max_exact, n, large)).astype(np.int32)


def _dil_bias_table(rel_bias_cols, r):
    w, qb, hg = DIL_W, DIL_QB, DIL_HG
    rel = np.arange(qb + 2 * w)[None, :] - w - np.arange(qb)[:, None]
    band = np.abs(rel) <= w
    bucket = _t5_bucket(np.clip(rel, -w, w) * r).reshape(-1)
    onehot = np.zeros((NUM_BUCKETS, bucket.size), np.float32)
    onehot[bucket, np.arange(bucket.size)] = 1.0
    bias = lax.dot_general(rel_bias_cols.T, jnp.asarray(onehot), (((1,), (0,)), ((), ())),
                           precision=lax.Precision.HIGHEST).reshape(DIL_HEADS, qb, qb + 2 * w)
    key = np.arange(qb + 2 * w)[None, :]
    tables = []
    for var in range(4):
        ok = band & ~((key < w) & bool(var & 1)) & ~((key >= qb + w) & bool(var & 2))
        tables.append(jnp.where(ok[None], bias, NEG).reshape(DIL_HEADS // hg, hg * qb, qb + 2 * w))
    return jnp.stack(tables) * LOG2E


def _dil_attn_kernel(bias_ref, q_ref, kp_ref, km_ref, kn_ref, vp_ref, vm_ref, vn_ref,
                     o_ref, lse_ref, kx_ref, vx_ref, *, sub_bounds):
    tq, w, qb, hg = DIL_TQ, DIL_W, DIL_QB, DIL_HG
    gl = hg * DIL_HD
    i = pl.program_id(1)
    kx_ref[0:w] = kp_ref[...]
    kx_ref[w:w + tq] = km_ref[...]
    kx_ref[w + tq:] = kn_ref[...]
    vx_ref[0:w] = vp_ref[...]
    vx_ref[w:w + tq] = vm_ref[...]
    vx_ref[w + tq:] = vn_ref[...]
    start = i * tq
    lo = jnp.int32(0)
    hi = jnp.int32(sub_bounds[1])
    for b0, b1 in zip(sub_bounds[1:-1], sub_bounds[2:]):
        inside = start >= b0
        lo = jnp.where(inside, b0, lo)
        hi = jnp.where(inside, b1, hi)
    head_of_lane = lax.broadcasted_iota(jnp.int32, (qb, gl), 1) // DIL_HD
    lane = lax.broadcasted_iota(jnp.int32, (qb, LANES), 1)

    for jb in range(tq // qb):
        rows = slice(jb * qb, (jb + 1) * qb)
        krows = slice(jb * qb, (jb + 1) * qb + 2 * w)
        q0 = start + jb * qb
        var = (q0 == lo).astype(jnp.int32) + 2 * (q0 + qb == hi).astype(jnp.int32)
        lse_acc = jnp.zeros((qb, LANES), F32)
        for p in range(DIL_HEADS // hg):
            sl = slice(p * gl, (p + 1) * gl)
            qg = q_ref[rows, sl]
            zero = jnp.zeros_like(qg)
            qs = jnp.concatenate([jnp.where(head_of_lane == h, qg, zero) for h in range(hg)], axis=0)
            s = _dot_nt(qs, kx_ref[krows, sl]) + bias_ref[var, p]
            mx = jnp.max(s, axis=-1, keepdims=True)
            pr = jnp.exp2(s - mx)
            l = jnp.sum(pr, axis=-1, keepdims=True)
            pv = _dot(pr.astype(BF16), vx_ref[krows, sl])
            lse = mx + jnp.log2(l)
            og = pv[:qb]
            for h in range(hg):
                if h:
                    og = jnp.where(head_of_lane == h, pv[h * qb:(h + 1) * qb], og)
                hs = slice(h * qb, (h + 1) * qb)
                lse_acc = jnp.where(lane == p * hg + h, lse[hs],
                                    jnp.where(lane == DIL_HEADS + p * hg + h, mx[hs], lse_acc))
            o_ref[rows, sl] = og
        lse_ref[rows, :] = lse_acc


def _dil_attn(qkv, bias, r, seq_bounds):
    rows = qkv.shape[1]
    tq, w = DIL_TQ, DIL_W
    n_blk = rows // tq
    n_w = rows // w
    per = tq // w
    sub_bounds = tuple(b // r for b in seq_bounds)
    assert all(b % tq == 0 for b in sub_bounds)

    def main(col):
        return pl.BlockSpec((None, tq, D_MODEL), lambda rho, i: (rho, i, col))

    def prev(col):
        return pl.BlockSpec((None, w, D_MODEL), lambda rho, i: (rho, jnp.maximum(i * per - 1, 0), col))

    def nxt(col):
        return pl.BlockSpec((None, w, D_MODEL), lambda rho, i: (rho, jnp.minimum((i + 1) * per, n_w - 1), col))

    return pl.pallas_call(
        functools.partial(_dil_attn_kernel, sub_bounds=sub_bounds),
        grid=(r, n_blk),
        in_specs=[_const_spec(bias.shape), main(0), prev(1), main(1), nxt(1), prev(2), main(2), nxt(2)],
        out_specs=[pl.BlockSpec((None, tq, D_MODEL), lambda rho, i: (rho, i, 0)),
                   pl.BlockSpec((None, tq, LANES), lambda rho, i: (rho, i, 0))],
        out_shape=[jax.ShapeDtypeStruct((r, rows, D_MODEL), F32),
                   jax.ShapeDtypeStruct((r, rows, LANES), F32)],
        scratch_shapes=[pltpu.VMEM((tq + 2 * w, D_MODEL), BF16)] * 2,
        compiler_params=_params(2),
        name=f"dil_attn_r{r}",
    )(bias, qkv, qkv, qkv, qkv, qkv, qkv, qkv)


def _dil_out_kernel(x_ref, o0_ref, o1_ref, o2_ref, l0_ref, l1_ref, l2_ref, ex_ref, w_ref,
                    gc_ref, wq_ref, k_ref, v_ref, wo_ref, out_ref, os_ref, ls_ref):
    def token_order(src_ref, dst_ref, r):
        if r == 1:
            return src_ref[0]
        n_chunk = src_ref.shape[-1] // LANES
        for rho in range(r):
            for j in range(n_chunk):
                dst_ref[j, pl.ds(rho, TM_OUT // r, stride=r), :] = src_ref[rho, :, j * LANES:(j + 1) * LANES]
        return jnp.concatenate([dst_ref[j] for j in range(n_chunk)], axis=1)

    rs = [r for _, r in DIL_GROUPS]
    stats = [token_order(l_ref, ls_ref.at[g], r) for g, (l_ref, r) in enumerate(zip((l0_ref, l1_ref, l2_ref), rs))]
    top = jnp.maximum(jnp.maximum(stats[0], stats[1]), stats[2])
    inv = 1.0 / (jnp.exp2(stats[0] - top) + jnp.exp2(stats[1] - top) + jnp.exp2(stats[2] - top))
    ex = ex_ref[...]
    is_head_lane = lax.broadcasted_iota(jnp.int32, (TM_OUT, LANES), 1) < DIL_HEADS
    acc = None
    for g, (st, o_ref, r) in enumerate(zip(stats, (o0_ref, o1_ref, o2_ref), rs)):
        wt = jnp.exp2(pltpu.roll(st, LANES - DIL_HEADS, 1) - top) * inv
        wt = jnp.where(is_head_lane, wt, 0.0)
        hi = wt.astype(BF16)
        lo = (wt - hi.astype(F32)).astype(BF16)
        term = _dot(jnp.concatenate([hi, lo], axis=1), ex) * token_order(o_ref, os_ref, r)
        acc = term if acc is None else acc + term
    x = x_ref[...] + _dot(acc.astype(BF16), w_ref[...])
    out_ref[...] = _cross_body(x, gc_ref, wq_ref, k_ref, v_ref, wo_ref)


def _dil_out(x, outs, lses, w_out, cross_args, seq_bounds):
    t = x.shape[0]
    expand = np.zeros((2 * LANES, D_MODEL), np.float32)
    for h in range(DIL_HEADS):
        expand[h, h * DIL_HD:(h + 1) * DIL_HD] = 1.0
        expand[LANES + h, h * DIL_HD:(h + 1) * DIL_HD] = 1.0
    row = pl.BlockSpec((TM_OUT, D_MODEL), lambda i: (i, 0))
    res = lambda n: [pl.BlockSpec((r, TM_OUT // r, n), lambda i: (0, i, 0)) for _, r in DIL_GROUPS]
    return pl.pallas_call(
        _dil_out_kernel,
        grid=(t // TM_OUT,),
        in_specs=[row] + res(D_MODEL) + res(LANES) + [_const_spec((2 * LANES, D_MODEL)), _const_spec((D_MODEL, D_MODEL))]
        + _cross_specs(seq_bounds),
        out_specs=row,
        out_shape=jax.ShapeDtypeStruct((t, D_MODEL), F32),
        scratch_shapes=[pltpu.VMEM((D_MODEL // LANES, TM_OUT, LANES), F32),
                        pltpu.VMEM((N_DIL, 1, TM_OUT, LANES), F32)],
        compiler_params=_params(1),
        name="dil_out_cross",
    )(x, *outs, *lses, jnp.asarray(expand, BF16), w_out, *cross_args)


def kernel(x_prompt, x_sample, mem_prompt, mem_sample, rel_bias, norm_ffn1, ffn1_in, ffn1_out, norm_mix,
           gla_w_in, gla_wg_f, gla_bg_f, gla_wg_b, gla_bg_b, gla_norm, gla_w_out, dil_w_qkv, dil_w_out,
           norm_cross, norm_mem, cross_w_q, cross_w_kv, cross_w_o, norm_ffn2, ffn2_in, ffn2_out, norm_final):
    bp, sp, _ = x_prompt.shape
    bs, ss, _ = x_sample.shape
    depth = norm_ffn1.shape[0]
    xs = [x_prompt.reshape(bp * sp, D_MODEL), x_sample.reshape(bs * ss, D_MODEL)]
    t = bp * sp + bs * ss
    mem = jnp.concatenate([mem_prompt, mem_sample], axis=0)
    seq_bounds = tuple(i * sp for i in range(bp)) + tuple(bp * sp + i * ss for i in range(bs + 1))
    row = lambda v: v.reshape(1, -1)
    bf = lambda v: v.astype(BF16)
    dq = GLA_HEADS * GLA_DK
    dv = GLA_HEADS * GLA_DV

    for i in range(depth):
        x, = _ffn(xs, row(norm_ffn1[i]), bf(ffn1_in[i]), bf(ffn1_out[i]), row(norm_final), False, [t])
        k_mem, v_mem = _cross_kv(mem, row(norm_mem[i]), bf(cross_w_kv[i]))
        cross_args = (row(norm_cross[i]), bf(cross_w_q[i]), k_mem, v_mem, bf(cross_w_o[i]))
        j = i // 2
        if i % 2 == 0:
            w_in = gla_w_in[j]
            zero = jnp.zeros((GLA_RANK, dq), F32)
            w_g = jnp.concatenate([jnp.concatenate([gla_wg_f[j], zero], axis=1),
                                   jnp.concatenate([zero, gla_wg_b[j]], axis=1)], axis=0)
            b_g = jnp.concatenate([gla_bg_f[j], gla_bg_b[j]]).reshape(1, 2 * dq)
            q, k, v, gate, gf, gb = _gla_proj(x, row(norm_mix[i]), bf(w_in[:, :2 * dq + 2 * dv]),
                                              bf(w_in[:, 2 * dq + 2 * dv:]), bf(w_g), b_g)
            o_f, o_b = _gla_scan(q, k, v, gf, gb, seq_bounds)
            x = _gla_out(x, o_f, o_b, gate, row(gla_norm[j]), bf(gla_w_out[j]), cross_args, seq_bounds)
        else:
            qkvs = _dil_proj(x, row(norm_mix[i]), bf(dil_w_qkv[j]))
            outs, lses = [], []
            for g, (_, r) in enumerate(DIL_GROUPS):
                bias = _dil_bias_table(rel_bias[:, g * DIL_HEADS:(g + 1) * DIL_HEADS], r)
                o, lse = _dil_attn(qkvs[g], bias, r, seq_bounds)
                outs.append(o)
                lses.append(lse)
            x = _dil_out(x, outs, lses, bf(dil_w_out[j]), cross_args, seq_bounds)
        last = i == depth - 1
        xs = _ffn([x], row(norm_ffn2[i]), bf(ffn2_in[i]), bf(ffn2_out[i]), row(norm_final), last,
                  [bp * sp, bs * ss] if last else [t])

    y_prompt, y_sample = xs
    return (y_prompt.reshape(bp, sp, D_MODEL), y_sample.reshape(bs, ss, D_MODEL))
```
